```python
import jax, jax.numpy as jnp
from jax import lax
import numpy as np

D_MODEL = 2048
BATCH = 4
SEQ = 8192
DEPTH = 4

HEAD_DIM = 128
N_HEADS_A = 12
N_HEADS_B = 12
WIDTH_A = N_HEADS_A * HEAD_DIM
WIDTH_B = N_HEADS_B * HEAD_DIM
DILATED_PATTERNS = ((128, 1), (512, 4), (2048, 16))
Q_BLOCK = 128
ROPE_THETA = 10000.0
LN_EPS = 1e-5
DEEPNORM_ALPHA = float((2 * DEPTH) ** 0.25)
DEEPNORM_BETA = float((8 * DEPTH) ** -0.25)
FORGET_BIAS_INIT = 2.0

SPLIT_SIZES = (WIDTH_A, WIDTH_A, WIDTH_A, WIDTH_A,
               WIDTH_B, WIDTH_B, WIDTH_B, WIDTH_B,
               N_HEADS_B,
               2 * D_MODEL)
SPLIT_POINTS = tuple(int(v) for v in np.cumsum(SPLIT_SIZES)[:-1])
N_IN_COLS = int(sum(SPLIT_SIZES))

kernel_name = "hybrid_dilated_forgetting_attention_deepnorm"


def rope(t, pos):
    half = HEAD_DIM // 2
    inv_freq = ROPE_THETA ** (-jnp.arange(half, dtype=jnp.float32) / half)
    ang = pos.astype(jnp.float32)[:, None] * inv_freq[None, :]
    cos = jnp.cos(ang)[None, :, None, :]
    sin = jnp.sin(ang)[None, :, None, :]
    t32 = t.astype(jnp.float32)
    t1, t2 = t32[..., :half], t32[..., half:]
    return jnp.concatenate([t1 * cos - t2 * sin, t1 * sin + t2 * cos], axis=-1).astype(t.dtype)


def dilated_pattern(q, k, v, window, dilation):
    B, S, H, hd = q.shape
    span = window // dilation
    L = S // dilation
    nblk = -(-L // Q_BLOCK)
    Lp = nblk * Q_BLOCK

    def to_residue(t):
        t = t.reshape(B, L, dilation, H, hd).transpose(0, 2, 1, 3, 4)
        return jnp.pad(t, ((0, 0), (0, 0), (0, Lp - L), (0, 0), (0, 0)))

    def band_keys(t):
        tp = jnp.pad(t, ((0, 0), (0, 0), (Q_BLOCK, 0), (0, 0), (0, 0)))
        tp = tp.reshape(B, dilation, nblk + 1, Q_BLOCK, H, hd)
        return jnp.concatenate([tp[:, :, :-1], tp[:, :, 1:]], axis=3)

    qb = to_residue(q).reshape(B, dilation, nblk, Q_BLOCK, H, hd)
    kb = band_keys(to_residue(k))
    vb = band_keys(to_residue(v))

    n_idx = jnp.arange(nblk)[:, None, None]
    i_idx = jnp.arange(Q_BLOCK)[None, :, None]
    j_idx = jnp.arange(2 * Q_BLOCK)[None, None, :]
    dist = Q_BLOCK + i_idx - j_idx
    valid = (dist >= 0) & (dist <= span) & (n_idx * Q_BLOCK + j_idx - Q_BLOCK >= 0)
    valid = valid[None, None, :, None]

    scale = HEAD_DIM ** -0.5
    s = jnp.einsum('bdnihe,bdnjhe->bdnhij', qb, kb, preferred_element_type=jnp.float32) * scale
    s = jnp.where(valid, s, -jnp.inf)
    m = jnp.max(s, axis=-1, keepdims=True)
    p = jnp.exp(s - m)
    den = jnp.sum(p, axis=-1, keepdims=True)
    out = jnp.einsum('bdnhij,bdnjhe->bdnihe', p, vb.astype(jnp.float32))
    out = out / jnp.moveaxis(den, 3, 4)
    lse = jnp.moveaxis((m + jnp.log(den))[..., 0], 3, 4)

    out = out.reshape(B, dilation, Lp, H, hd)[:, :, :L].transpose(0, 2, 1, 3, 4).reshape(B, S, H, hd)
    lse = lse.reshape(B, dilation, Lp, H)[:, :, :L].transpose(0, 2, 1, 3).reshape(B, S, H)
    return out, lse


def dilated_mixture(q, k, v):
    outs, lses = [], []
    for window, dilation in DILATED_PATTERNS:
        o, l = dilated_pattern(q, k, v, window, dilation)
        outs.append(o)
        lses.append(l)
    w = jax.nn.softmax(jnp.stack(lses, axis=0), axis=0)
    return jnp.sum(w[..., None] * jnp.stack(outs, axis=0), axis=0).astype(q.dtype)


def forgetting_attention(q, k, v, log_f):
    B, S, H, hd = q.shape
    nblk = S // Q_BLOCK
    cum = jnp.cumsum(log_f, axis=1).transpose(0, 2, 1)
    q_blocks = q.reshape(B, nblk, Q_BLOCK, H, hd).transpose(1, 0, 2, 3, 4)
    c_blocks = cum.reshape(B, H, nblk, Q_BLOCK).transpose(2, 0, 1, 3)
    kpos = jnp.arange(S)
    scale = HEAD_DIM ** -0.5

    def block(args):
        n, qn, cn = args
        s = jnp.einsum('bihe,bjhe->bhij', qn, k, preferred_element_type=jnp.float32) * scale
        s = s + cn[..., :, None] - cum[..., None, :]
        qpos = n * Q_BLOCK + jnp.arange(Q_BLOCK)
        causal = kpos[None, :] <= qpos[:, None]
        s = jnp.where(causal[None, None], s, -jnp.inf)
        p = jax.nn.softmax(s, axis=-1)
        return jnp.einsum('bhij,bjhe->bihe', p, v.astype(jnp.float32)).astype(q.dtype)

    out = lax.map(block, (jnp.arange(nblk), q_blocks, c_blocks))
    return out.transpose(1, 0, 2, 3, 4).reshape(B, S, H, hd)


def layer_norm(x, g, b):
    x32 = x.astype(jnp.float32)
    mu = jnp.mean(x32, axis=-1, keepdims=True)
    var = jnp.mean(jnp.square(x32 - mu), axis=-1, keepdims=True)
    y = (x32 - mu) * lax.rsqrt(var + LN_EPS)
    return (y * g.astype(jnp.float32) + b.astype(jnp.float32)).astype(x.dtype)


def hybrid_layer(x, pos, w_in, b_forget, b_gate, w_up_a, w_up_b, w_out, ln_g, ln_b):
    B, S, _ = x.shape
    h = jnp.einsum('bsd,dc->bsc', x, w_in)
    qa, ka, va, za, qb, kb, vb, zb, f_logit, g_logit = jnp.split(h, SPLIT_POINTS, axis=-1)
    heads = lambda t, n: t.reshape(B, S, n, HEAD_DIM)

    qa = rope(heads(qa, N_HEADS_A), pos)
    ka = rope(heads(ka, N_HEADS_A), pos)
    out_a = dilated_mixture(qa, ka, heads(va, N_HEADS_A)).reshape(B, S, WIDTH_A)
    up_a = jnp.einsum('bsw,wd->bsd', out_a * jax.nn.silu(za), w_up_a)

    log_f = jax.nn.log_sigmoid((f_logit + b_forget).astype(jnp.float32))
    out_b = forgetting_attention(heads(qb, N_HEADS_B), heads(kb, N_HEADS_B),
                                 heads(vb, N_HEADS_B), log_f).reshape(B, S, WIDTH_B)
    up_b = jnp.einsum('bsw,wd->bsd', out_b * jax.nn.silu(zb), w_up_b)

    gates = jax.nn.sigmoid(g_logit + b_gate)
    g_a, g_b = gates[..., :D_MODEL], gates[..., D_MODEL:]
    y = jnp.einsum('bsd,de->bse', g_a * up_a + g_b * up_b, w_out)

    return layer_norm(DEEPNORM_ALPHA * x + y, ln_g, ln_b)


def setup_inputs(seed: int = 0) -> dict:
    key = jax.random.key(seed)
    ks = jax.random.split(key, 10)
    x = jax.random.normal(ks[0], (BATCH, SEQ, D_MODEL), jnp.float32)

    offs = (0,) + SPLIT_POINTS
    col_scale = jnp.ones((N_IN_COLS,), jnp.float32)
    col_scale = col_scale.at[offs[2]:offs[3]].set(DEEPNORM_BETA)
    col_scale = col_scale.at[offs[6]:offs[7]].set(DEEPNORM_BETA)
    w_in = jax.random.normal(ks[1], (DEPTH, D_MODEL, N_IN_COLS), jnp.float32) * (D_MODEL ** -0.5) * col_scale
    b_forget = FORGET_BIAS_INIT + 0.1 * jax.random.normal(ks[2], (DEPTH, N_HEADS_B), jnp.float32)
    b_gate = 0.02 * jax.random.normal(ks[3], (DEPTH, 2 * D_MODEL), jnp.float32)
    w_up_a = jax.random.normal(ks[4], (DEPTH, WIDTH_A, D_MODEL), jnp.float32) * (WIDTH_A ** -0.5) * DEEPNORM_BETA
    w_up_b = jax.random.normal(ks[5], (DEPTH, WIDTH_B, D_MODEL), jnp.float32) * (WIDTH_B ** -0.5) * DEEPNORM_BETA
    w_out = jax.random.normal(ks[6], (DEPTH, D_MODEL, D_MODEL), jnp.float32) * (D_MODEL ** -0.5) * DEEPNORM_BETA
    ln_g = 1.0 + 0.02 * jax.random.normal(ks[7], (DEPTH, D_MODEL), jnp.float32)
    ln_b = 0.02 * jax.random.normal(ks[8], (DEPTH, D_MODEL), jnp.float32)
    return {"x": x, "w_in": w_in, "b_forget": b_forget, "b_gate": b_gate,
            "w_up_a": w_up_a, "w_up_b": w_up_b, "w_out": w_out,
            "ln_g": ln_g, "ln_b": ln_b}


def reference(x, w_in, b_forget, b_gate, w_up_a, w_up_b, w_out, ln_g, ln_b):
    pos = jnp.arange(x.shape[1], dtype=jnp.int32)
    for l in range(DEPTH):
        x = hybrid_layer(x, pos, w_in[l], b_forget[l], b_gate[l],
                         w_up_a[l], w_up_b[l], w_out[l], ln_g[l], ln_b[l])
    return x
```

```python
import functools
import math

import jax
import jax.numpy as jnp
from jax import lax
from jax.experimental import pallas as pl
from jax.experimental.pallas import tpu as pltpu

HEAD_DIM = 128
Q_BLOCK = 128
DILATED_PATTERNS = ((128, 1), (512, 4), (2048, 16))
ROPE_THETA = 10000.0
LN_EPS = 1e-5
LANES = 128
NEG_INF = float("-inf")

_F32 = jnp.float32
_BF16 = jnp.bfloat16


def _pick(n, candidates):
    for c in candidates:
        if n % c == 0:
            return c
    raise ValueError(f"no tile in {candidates} divides {n}")


def _params(vmem_mib, n_axes):
    return pltpu.CompilerParams(dimension_semantics=("arbitrary",) * n_axes,
                                vmem_limit_bytes=vmem_mib * 1024 * 1024)


def _proj_kernel(kind, rope_tiles, x_ref, w_ref, *rest):
    o_ref = rest[-1]
    acc = jnp.dot(x_ref[...], w_ref[...], preferred_element_type=_F32)
    if kind == "rope":
        cos_ref, sin_ref, scale_ref = rest[:3]

        @pl.when(pl.program_id(1) < rope_tiles)
        def _():
            cos, sin = cos_ref[...], sin_ref[...]
            for c in range(acc.shape[1] // HEAD_DIM):
                t = acc[:, c * HEAD_DIM:(c + 1) * HEAD_DIM]
                r = t * cos + pltpu.roll(t, HEAD_DIM // 2, axis=1) * sin
                o_ref[:, c * HEAD_DIM:(c + 1) * HEAD_DIM] = (
                    r * scale_ref[:, c * HEAD_DIM:(c + 1) * HEAD_DIM]).astype(o_ref.dtype)

        @pl.when(pl.program_id(1) >= rope_tiles)
        def _():
            o_ref[...] = acc.astype(o_ref.dtype)
    elif kind == "linear":
        o_ref[...] = (acc * rest[0][...]).astype(o_ref.dtype)
    elif kind == "silu":
        o_ref[...] = (acc * jax.nn.sigmoid(acc)).astype(o_ref.dtype)
    elif kind == "gate":
        o_ref[...] = jax.nn.sigmoid(acc + rest[0][...]).astype(o_ref.dtype)
    else:
        raise ValueError(kind)


def _project(kind, xb, w, extras, out_dtype, seq, rope_cols=0):
    m, k = xb.shape
    n = w.shape[1]
    tm = _pick(seq, (1024, 512, 256, 128))
    tn = _pick(math.gcd(n, rope_cols), (512, 256, 128))
    in_specs = [pl.BlockSpec((tm, k), lambda i, j: (i, 0)),
                pl.BlockSpec((k, tn), lambda i, j: (0, j))]
    if kind == "rope":
        nrow = seq // tm
        table = pl.BlockSpec((tm, HEAD_DIM), lambda i, j: (i % nrow, 0))
        in_specs += [table, table, pl.BlockSpec((1, tn), lambda i, j: (0, j))]
    else:
        in_specs += [pl.BlockSpec((1, tn), lambda i, j: (0, j)) for _ in extras]
    return pl.pallas_call(
        functools.partial(_proj_kernel, kind, rope_cols // tn),
        grid=(m // tm, n // tn),
        in_specs=in_specs,
        out_specs=pl.BlockSpec((tm, tn), lambda i, j: (i, j)),
        out_shape=jax.ShapeDtypeStruct((m, n), out_dtype),
        compiler_params=_params(40, 2),
    )(xb, w, *extras)


def _forget_kernel(tiles_per_seq, x_ref, w_ref, b_ref, tri_ref, o_ref, carry_ref):
    @pl.when(pl.program_id(0) % tiles_per_seq == 0)
    def _():
        carry_ref[...] = jnp.zeros_like(carry_ref)

    z = jnp.dot(x_ref[...], w_ref[...], preferred_element_type=_F32) + b_ref[...]
    log_f = jnp.minimum(z, 0.0) - jnp.log1p(jnp.exp(-jnp.abs(z)))
    hi = log_f.astype(_BF16)
    r1 = log_f - hi.astype(_F32)
    mid = r1.astype(_BF16)
    lo = (r1 - mid.astype(_F32)).astype(_BF16)
    tri = tri_ref[...]
    cum = (jnp.dot(tri, hi, preferred_element_type=_F32)
           + jnp.dot(tri, mid, preferred_element_type=_F32)
           + jnp.dot(tri, lo, preferred_element_type=_F32)) + carry_ref[...]
    o_ref[...] = cum
    carry_ref[...] = cum[-1:, :]


def _forget_cumsum(xb, w_f, b_f, seq):
    m, k = xb.shape
    tm = _pick(seq, (512, 256, 128))
    tri = (lax.broadcasted_iota(jnp.int32, (tm, tm), 0)
           >= lax.broadcasted_iota(jnp.int32, (tm, tm), 1)).astype(_BF16)
    return pl.pallas_call(
        functools.partial(_forget_kernel, seq // tm),
        grid=(m // tm,),
        in_specs=[pl.BlockSpec((tm, k), lambda i: (i, 0)),
                  pl.BlockSpec((k, LANES), lambda i: (0, 0)),
                  pl.BlockSpec((1, LANES), lambda i: (0, 0)),
                  pl.BlockSpec((tm, tm), lambda i: (0, 0))],
        out_specs=pl.BlockSpec((tm, LANES), lambda i: (i, 0)),
        out_shape=jax.ShapeDtypeStruct((m, LANES), _F32),
        scratch_shapes=[pltpu.VMEM((1, LANES), _F32)],
        compiler_params=_params(24, 1),
    )(xb, w_f, b_f, tri)


def _dilated_kernel(tile, q_ref, k_ref, v_ref, o_ref, acc_ref, m_ref, l_ref):
    t0 = pl.program_id(2) * tile
    row = lax.broadcasted_iota(jnp.int32, (Q_BLOCK, 2 * Q_BLOCK), 0)
    col = lax.broadcasted_iota(jnp.int32, (Q_BLOCK, 2 * Q_BLOCK), 1)
    in_prev = (col < Q_BLOCK) & (col >= row)
    in_cur = (col >= Q_BLOCK) & (col - Q_BLOCK <= row)

    for g, (_, d) in enumerate(DILATED_PATTERNS):
        def rows(start, d=d):
            return pl.ds(start, Q_BLOCK) if d == 1 else pl.ds(start, Q_BLOCK, stride=d)

        def block(i, carry, g=g, d=d, rows=rows):
            n, r = i // d, i % d
            q0 = n * (Q_BLOCK * d) + r
            cur0 = t0 + q0
            has_prev = cur0 >= Q_BLOCK * d
            prev0 = jnp.where(has_prev, cur0 - Q_BLOCK * d, cur0)
            q = q_ref[0, rows(q0), :].astype(_BF16)
            kk = jnp.concatenate([k_ref[0, rows(prev0), :], k_ref[0, rows(cur0), :]], axis=0).astype(_BF16)
            vv = jnp.concatenate([v_ref[0, rows(prev0), :], v_ref[0, rows(cur0), :]], axis=0).astype(_BF16)
            s = lax.dot_general(q, kk, (((1,), (1,)), ((), ())), preferred_element_type=_F32)
            s = jnp.where((in_prev & has_prev) | in_cur, s, NEG_INF)
            m = jnp.max(s, axis=-1, keepdims=True)
            p = jnp.exp(s - m)
            acc_ref[g, rows(q0), :] = jnp.dot(p.astype(_BF16), vv, preferred_element_type=_F32)
            m_ref[g, rows(q0), :] = m
            l_ref[g, rows(q0), :] = jnp.sum(p, axis=-1, keepdims=True)
            return carry

        lax.fori_loop(0, tile // Q_BLOCK, block, 0)

    m_all = jnp.maximum(jnp.maximum(m_ref[0], m_ref[1]), m_ref[2])
    num = jnp.zeros((tile, HEAD_DIM), _F32)
    den = jnp.zeros((tile, 1), _F32)
    for g in range(len(DILATED_PATTERNS)):
        e = jnp.exp(m_ref[g] - m_all)
        num = num + e * acc_ref[g]
        den = den + e * l_ref[g]
    o_ref[0] = (num / den).astype(o_ref.dtype)


def _dilated_attention(qkv, n_heads):
    b, s, _ = qkv.shape
    tile = Q_BLOCK * DILATED_PATTERNS[-1][1]
    assert s % tile == 0
    n_pat = len(DILATED_PATTERNS)
    whole = lambda part: pl.BlockSpec((1, s, HEAD_DIM), lambda bi, h, t: (bi, 0, part * n_heads + h))
    tiled = pl.BlockSpec((1, tile, HEAD_DIM), lambda bi, h, t: (bi, t, h))
    return pl.pallas_call(
        functools.partial(_dilated_kernel, tile),
        grid=(b, n_heads, s // tile),
        in_specs=[tiled, whole(1), whole(2)],
        out_specs=tiled,
        out_shape=jax.ShapeDtypeStruct((b, s, n_heads * HEAD_DIM), _BF16),
        scratch_shapes=[pltpu.VMEM((n_pat, tile, HEAD_DIM), _F32),
                        pltpu.VMEM((n_pat, tile, 1), _F32),
                        pltpu.VMEM((n_pat, tile, 1), _F32)],
        compiler_params=_params(48, 3),
    )(qkv, qkv, qkv)


def _fox_kernel(tq, q_ref, k_ref, v_ref, c_ref, o_ref, m_ref, l_ref, acc_ref):
    qi = pl.program_id(2)
    q = q_ref[0]
    c_first = c_ref[0, 0, :, pl.ds(pl.multiple_of(qi * tq, tq), LANES)][:, :1]
    m_ref[...] = jnp.full_like(m_ref, NEG_INF)
    l_ref[...] = jnp.zeros_like(l_ref)
    acc_ref[...] = jnp.zeros_like(acc_ref)

    def step(j, masked):
        k0 = pl.multiple_of(j * tq, tq)
        kk = k_ref[0, pl.ds(k0, tq), :]
        vv = v_ref[0, pl.ds(k0, tq), :]
        s = lax.dot_general(q, kk, (((1,), (1,)), ((), ())), preferred_element_type=_F32)
        s = s + (c_first - c_ref[0, 0, :, pl.ds(k0, tq)])
        if masked:
            row = lax.broadcasted_iota(jnp.int32, (tq, tq), 0)
            col = lax.broadcasted_iota(jnp.int32, (tq, tq), 1)
            s = jnp.where(col <= row, s, NEG_INF)
        m_old = m_ref[...]
        m_new = jnp.maximum(m_old, jnp.max(s, axis=-1, keepdims=True))
        a = jnp.exp(m_old - m_new)
        p = jnp.exp(s - m_new)
        l_ref[...] = a * l_ref[...] + jnp.sum(p, axis=-1, keepdims=True)
        acc_ref[...] = a * acc_ref[...] + jnp.dot(p.astype(_BF16), vv, preferred_element_type=_F32)
        m_ref[...] = m_new

    def body(j, carry):
        step(j, False)
        return carry

    lax.fori_loop(0, qi, body, 0)
    step(qi, True)
    o_ref[0] = (acc_ref[...] / l_ref[...]).astype(o_ref.dtype)


def _forgetting_attention(qkv, cum_row, n_heads):
    b, s, _ = qkv.shape
    tq = _pick(s, (512, 256, 128))
    whole = lambda part: pl.BlockSpec((1, s, HEAD_DIM), lambda bi, h, t: (bi, 0, part * n_heads + h))
    tiled = pl.BlockSpec((1, tq, HEAD_DIM), lambda bi, h, t: (bi, t, h))
    return pl.pallas_call(
        functools.partial(_fox_kernel, tq),
        grid=(b, n_heads, s // tq),
        in_specs=[tiled, whole(1), whole(2),
                  pl.BlockSpec((1, 1, 1, s), lambda bi, h, t: (bi, h, 0, 0))],
        out_specs=tiled,
        out_shape=jax.ShapeDtypeStruct((b, s, n_heads * HEAD_DIM), _BF16),
        scratch_shapes=[pltpu.VMEM((tq, 1), _F32),
                        pltpu.VMEM((tq, 1), _F32),
                        pltpu.VMEM((tq, HEAD_DIM), _F32)],
        compiler_params=_params(40, 3),
    )(qkv, qkv, qkv, cum_row)


def _merge_kernel(alpha, d_model, oa_ref, za_ref, ob_ref, zb_ref, g_ref, x_ref,
                  wa_ref, wb_ref, wo_ref, lng_ref, lnb_ref, o_ref, ob16_ref):
    up_a = jnp.dot(oa_ref[...] * za_ref[...], wa_ref[...], preferred_element_type=_F32)
    up_b = jnp.dot(ob_ref[...] * zb_ref[...], wb_ref[...], preferred_element_type=_F32)
    mix = (g_ref[:, :d_model].astype(_F32) * up_a + g_ref[:, d_model:].astype(_F32) * up_b)
    y = jnp.dot(mix.astype(_BF16), wo_ref[...], preferred_element_type=_F32)
    r = alpha * x_ref[...] + y
    mu = jnp.mean(r, axis=-1, keepdims=True)
    cen = r - mu
    var = jnp.mean(cen * cen, axis=-1, keepdims=True)
    out = cen * lax.rsqrt(var + LN_EPS) * lng_ref[...] + lnb_ref[...]
    o_ref[...] = out
    ob16_ref[...] = out.astype(_BF16)


def _merge(oa, sza, ob, szb, gates, x, w_up_a, w_up_b, w_out, ln_g, ln_b, alpha):
    m, d_model = x.shape
    w = oa.shape[1]
    tm = _pick(m, (256, 128))
    row = lambda n: pl.BlockSpec((tm, n), lambda i: (i, 0))
    const = lambda a, c: pl.BlockSpec((a, c), lambda i: (0, 0), pipeline_mode=pl.Buffered(1))
    return pl.pallas_call(
        functools.partial(_merge_kernel, alpha, d_model),
        grid=(m // tm,),
        in_specs=[row(w), row(w), row(w), row(w), row(2 * d_model), row(d_model),
                  const(w, d_model), const(w, d_model), const(d_model, d_model),
                  const(1, d_model), const(1, d_model)],
        out_specs=[row(d_model), row(d_model)],
        out_shape=[jax.ShapeDtypeStruct((m, d_model), _F32),
                   jax.ShapeDtypeStruct((m, d_model), _BF16)],
        compiler_params=_params(48, 1),
    )(oa, sza, ob, szb, gates, x, w_up_a, w_up_b, w_out, ln_g, ln_b)


def _rope_tables(seq):
    half = HEAD_DIM // 2
    inv_freq = ROPE_THETA ** (-jnp.arange(half, dtype=_F32) / half)
    ang = jnp.arange(seq, dtype=_F32)[:, None] * inv_freq[None, :]
    cos, sin = jnp.cos(ang), jnp.sin(ang)
    return jnp.concatenate([cos, cos], axis=1), jnp.concatenate([-sin, sin], axis=1)


def _layer(x, xb, tables, w_in, b_forget, b_gate, w_up_a, w_up_b, w_out, ln_g, ln_b, dims):
    batch, seq, d_model, width_a, width_b, alpha = dims
    heads_a, heads_b = width_a // HEAD_DIM, width_b // HEAD_DIM
    m = batch * seq
    scale = HEAD_DIM ** -0.5
    o_ka, o_va, o_za = width_a, 2 * width_a, 3 * width_a
    o_qb = 4 * width_a
    o_kb, o_vb, o_zb, o_f = o_qb + width_b, o_qb + 2 * width_b, o_qb + 3 * width_b, o_qb + 4 * width_b
    o_g = o_f + heads_b
    wb = lambda lo, hi: w_in[:, lo:hi].astype(_BF16)
    ones = lambda n: jnp.ones((1, n), _F32)

    qk_scale = jnp.concatenate([jnp.full((1, width_a), scale, _F32), ones(2 * width_a)], axis=1)
    qkv_a = _project("rope", xb, wb(0, o_za), (*tables, qk_scale), _F32, seq, rope_cols=2 * width_a)
    sz_a = _project("silu", xb, wb(o_za, o_qb), (), _BF16, seq)
    out_a = _dilated_attention(qkv_a.reshape(batch, seq, 3 * width_a), heads_a)

    qkv_scale = jnp.concatenate([jnp.full((1, width_b), scale, _F32), ones(2 * width_b)], axis=1)
    qkv_b = _project("linear", xb, wb(o_qb, o_zb), (qkv_scale,), _BF16, seq)
    sz_b = _project("silu", xb, wb(o_zb, o_f), (), _BF16, seq)
    w_f = jnp.pad(w_in[:, o_f:o_g], ((0, 0), (0, LANES - heads_b))).astype(_BF16)
    b_f = jnp.pad(b_forget, (0, LANES - heads_b))[None, :]
    cum = _forget_cumsum(xb, w_f, b_f, seq)
    cum_row = cum.reshape(batch, seq, LANES)[:, :, :heads_b].transpose(0, 2, 1)[:, :, None, :]
    out_b = _forgetting_attention(qkv_b.reshape(batch, seq, 3 * width_b), cum_row, heads_b)

    gates = _project("gate", xb, wb(o_g, o_g + 2 * d_model), (b_gate[None, :],), _BF16, seq)
    return _merge(out_a.reshape(m, width_a), sz_a, out_b.reshape(m, width_b), sz_b, gates, x,
                  w_up_a.astype(_BF16), w_up_b.astype(_BF16), w_out.astype(_BF16),
                  ln_g[None, :], ln_b[None, :], alpha)


def kernel(x, w_in, b_forget, b_gate, w_up_a, w_up_b, w_out, ln_g, ln_b):
    batch, seq, d_model = x.shape
    depth = w_in.shape[0]
    width_a, width_b = w_up_a.shape[1], w_up_b.shape[1]
    alpha = float((2 * depth) ** 0.25)
    dims = (batch, seq, d_model, width_a, width_b, alpha)
    tables = _rope_tables(seq)
    x2 = x.reshape(batch * seq, d_model)
    xb = x2.astype(_BF16)
    for l in range(depth):
        x2, xb = _layer(x2, xb, tables, w_in[l], b_forget[l], b_gate[l],
                        w_up_a[l], w_up_b[l], w_out[l], ln_g[l], ln_b[l], dims)
    return x2.reshape(batch, seq, d_model)
```

```python
import functools
import math

import jax
import jax.numpy as jnp
from jax import lax
from jax.experimental import pallas as pl
from jax.experimental.pallas import tpu as pltpu

HEAD_DIM = 128
Q_BLOCK = 128
DILATED_PATTERNS = ((128, 1), (512, 4), (2048, 16))
ROPE_THETA = 10000.0
LN_EPS = 1e-5
LANES = 128
NEG_INF = float("-inf")
LOG2E = math.log2(math.e)
BIAS_PIECES = 3

_F32 = jnp.float32
_BF16 = jnp.bfloat16
_NT = (((1,), (1,)), ((), ()))


def _pick(n, candidates):
    for c in candidates:
        if n % c == 0:
            return c
    raise ValueError(f"no tile in {candidates} divides {n}")


def _params(vmem_mib, n_axes):
    return pltpu.CompilerParams(dimension_semantics=("arbitrary",) * n_axes,
                                vmem_limit_bytes=vmem_mib * 1024 * 1024)


def _lane_chunks(a):
    return [a[:, c * LANES:(c + 1) * LANES] for c in range(a.shape[1] // LANES)]


def _split_bf16(a):
    hi = a.astype(_BF16).astype(_F32)
    mid = (a - hi).astype(_BF16).astype(_F32)
    return hi, mid, a - hi - mid


def _proj_kernel(kind, rope_tiles, x_ref, w_ref, *rest):
    o_ref = rest[-1]
    acc = jnp.dot(x_ref[...], w_ref[0].astype(_BF16), preferred_element_type=_F32)
    if kind == "rope":
        cos_ref, sin_ref, scale_ref = rest[:3]

        @pl.when(pl.program_id(1) < rope_tiles)
        def _():
            cos, sin = cos_ref[...], sin_ref[...]
            for c, t in enumerate(_lane_chunks(acc)):
                r = t * cos + pltpu.roll(t, HEAD_DIM // 2, axis=1) * sin
                o_ref[:, c * LANES:(c + 1) * LANES] = (
                    r * scale_ref[:, c * LANES:(c + 1) * LANES]).astype(o_ref.dtype)

        @pl.when(pl.program_id(1) >= rope_tiles)
        def _():
            o_ref[...] = acc.astype(o_ref.dtype)
    elif kind == "linear":
        o_ref[...] = (acc * rest[0][...]).astype(o_ref.dtype)
    elif kind == "silu":
        o_ref[...] = (acc * jax.nn.sigmoid(acc)).astype(o_ref.dtype)
    elif kind == "gate":
        o_ref[...] = jax.nn.sigmoid(acc + rest[0][...]).astype(o_ref.dtype)
    else:
        raise ValueError(kind)


def _project(kind, xb, w, layer, col0, n, extras, out_dtype, seq, rope_cols=0):
    m, k = xb.shape
    tm = _pick(seq, (1024, 512, 256, 128))
    tn = _pick(math.gcd(math.gcd(n, rope_cols), col0), (512, 256, 128))
    j0 = col0 // tn
    in_specs = [pl.BlockSpec((tm, k), lambda i, j: (i, 0)),
                pl.BlockSpec((1, k, tn), lambda i, j: (layer, 0, j0 + j))]
    if kind == "rope":
        nrow = seq // tm
        table = pl.BlockSpec((tm, HEAD_DIM), lambda i, j: (i % nrow, 0))
        in_specs += [table, table, pl.BlockSpec((1, tn), lambda i, j: (0, j))]
    else:
        in_specs += [pl.BlockSpec((1, tn), lambda i, j: (0, j)) for _ in extras]
    return pl.pallas_call(
        functools.partial(_proj_kernel, kind, rope_cols // tn),
        grid=(m // tm, n // tn),
        in_specs=in_specs,
        out_specs=pl.BlockSpec((tm, tn), lambda i, j: (i, j)),
        out_shape=jax.ShapeDtypeStruct((m, n), out_dtype),
        compiler_params=_params(48, 2),
        name=f"proj_{kind}",
    )(xb, w, *extras)


def _forget_kernel(tiles_per_seq, n_heads, x_ref, w_ref, b_ref, tri_ref, o_ref, carry_ref):
    @pl.when(pl.program_id(0) % tiles_per_seq == 0)
    def _():
        carry_ref[...] = jnp.zeros_like(carry_ref)

    z = jnp.dot(x_ref[...], w_ref[...].astype(_BF16), preferred_element_type=_F32) + b_ref[...]
    log_f = jnp.minimum(z, 0.0) - jnp.log1p(jnp.exp(-jnp.abs(z)))
    tri = tri_ref[...]
    cum = carry_ref[...]
    for piece in _split_bf16(log_f):
        cum = cum + jnp.dot(tri, piece.astype(_BF16), preferred_element_type=_F32)
    carry_ref[...] = cum[-1:, :]
    lane = lax.broadcasted_iota(jnp.int32, cum.shape, 1)
    for h in range(n_heads):
        col = jnp.broadcast_to(cum[:, h:h + 1] * (-LOG2E), cum.shape)
        hi, mid, lo = _split_bf16(col)
        o_ref[0, h] = jnp.where(lane == 0, hi, jnp.where(lane == 1, mid, lo)).astype(_BF16)


def _forget_bias(xb, w_f, b_f, batch, seq, n_heads):
    m, k = xb.shape
    tm = _pick(seq, (512, 256, 128))
    tiles = seq // tm
    tri = (lax.broadcasted_iota(jnp.int32, (tm, tm), 0)
           >= lax.broadcasted_iota(jnp.int32, (tm, tm), 1)).astype(_BF16)
    return pl.pallas_call(
        functools.partial(_forget_kernel, tiles, n_heads),
        grid=(m // tm,),
        in_specs=[pl.BlockSpec((tm, k), lambda i: (i, 0)),
                  pl.BlockSpec((k, LANES), lambda i: (0, 0)),
                  pl.BlockSpec((1, LANES), lambda i: (0, 0)),
                  pl.BlockSpec((tm, tm), lambda i: (0, 0))],
        out_specs=pl.BlockSpec((1, n_heads, tm, LANES), lambda i: (i // tiles, 0, i % tiles, 0)),
        out_shape=jax.ShapeDtypeStruct((batch, n_heads, seq, LANES), _BF16),
        scratch_shapes=[pltpu.VMEM((1, LANES), _F32)],
        compiler_params=_params(32, 1),
        name="forget_bias",
    )(xb, w_f, b_f, tri)


DILATED_UNROLL = 4


def _dilated_kernel(tile, q_ref, k_ref, v_ref, o_ref, acc_ref, m_ref, l_ref):
    t0 = pl.program_id(2) * tile
    row = lax.broadcasted_iota(jnp.int32, (Q_BLOCK, LANES), 0)
    lane = lax.broadcasted_iota(jnp.int32, (Q_BLOCK, LANES), 1)
    in_prev = lane >= row
    in_cur = lane <= row

    for g, (_, d) in enumerate(DILATED_PATTERNS):
        def rows(start, d=d):
            return pl.ds(start, Q_BLOCK) if d == 1 else pl.ds(start, Q_BLOCK, stride=d)

        def block(i, carry, g=g, d=d, rows=rows):
            n, r = i // d, i % d
            q0 = n * (Q_BLOCK * d) + r
            cur0 = t0 + q0
            has_prev = cur0 >= Q_BLOCK * d
            prev0 = jnp.where(has_prev, cur0 - Q_BLOCK * d, cur0)
            q = q_ref[0, rows(q0), :].astype(_BF16)
            kk = jnp.concatenate([k_ref[0, rows(prev0), :], k_ref[0, rows(cur0), :]], axis=0).astype(_BF16)
            vv = jnp.concatenate([v_ref[0, rows(prev0), :], v_ref[0, rows(cur0), :]], axis=0).astype(_BF16)
            s_prev, s_cur = _lane_chunks(lax.dot_general(q, kk, _NT, preferred_element_type=_F32))
            s_prev = jnp.where(in_prev & has_prev, s_prev, NEG_INF)
            s_cur = jnp.where(in_cur, s_cur, NEG_INF)
            m = jnp.broadcast_to(jnp.max(jnp.maximum(s_prev, s_cur), axis=1, keepdims=True), s_cur.shape)
            p_prev, p_cur = jnp.exp2(s_prev - m), jnp.exp2(s_cur - m)
            p = jnp.concatenate([p_prev, p_cur], axis=1).astype(_BF16)
            acc_ref[g, rows(q0), :] = jnp.dot(p, vv, preferred_element_type=_F32)
            m_ref[g, rows(q0), :] = m
            l_ref[g, rows(q0), :] = p_prev + p_cur
            return carry

        lax.fori_loop(0, tile // Q_BLOCK, block, 0, unroll=DILATED_UNROLL)

    m_all = jnp.maximum(jnp.maximum(m_ref[0], m_ref[1]), m_ref[2])
    num = jnp.zeros((tile, HEAD_DIM), _F32)
    den = jnp.zeros((tile, LANES), _F32)
    for g in range(len(DILATED_PATTERNS)):
        e = jnp.exp2(m_ref[g] - m_all)
        num = num + e * acc_ref[g]
        den = den + e * l_ref[g]
    o_ref[0] = (num / jnp.sum(den, axis=1, keepdims=True)).astype(o_ref.dtype)


def _dilated_attention(qkv, n_heads):
    b, s, _ = qkv.shape
    tile = Q_BLOCK * DILATED_PATTERNS[-1][1]
    assert s % tile == 0
    n_pat = len(DILATED_PATTERNS)
    whole = lambda part: pl.BlockSpec((1, s, HEAD_DIM), lambda bi, h, t: (bi, 0, part * n_heads + h))
    tiled = pl.BlockSpec((1, tile, HEAD_DIM), lambda bi, h, t: (bi, t, h))
    stats = pltpu.VMEM((n_pat, tile, LANES), _F32)
    return pl.pallas_call(
        functools.partial(_dilated_kernel, tile),
        grid=(b, n_heads, s // tile),
        in_specs=[tiled, whole(1), whole(2)],
        out_specs=tiled,
        out_shape=jax.ShapeDtypeStruct((b, s, n_heads * HEAD_DIM), _BF16),
        scratch_shapes=[stats, stats, stats],
        compiler_params=_params(48, 3),
        name="dilated_attention",
    )(qkv, qkv, qkv)


def _fox_kernel(tq, q_ref, k_ref, v_ref, kx_ref, o_ref, m_ref, l_ref, acc_ref):
    qi = pl.program_id(2)
    row = lax.broadcasted_iota(jnp.int32, (tq, LANES), 0)
    lane = lax.broadcasted_iota(jnp.int32, (tq, LANES), 1)
    q_aug = jnp.concatenate([q_ref[0], (lane < BIAS_PIECES).astype(_BF16)], axis=1)
    m_ref[...] = jnp.full_like(m_ref, NEG_INF)
    l_ref[...] = jnp.zeros_like(l_ref)
    acc_ref[...] = jnp.zeros_like(acc_ref)

    def step(j, masked):
        k0 = pl.multiple_of(j * tq, tq)
        k_aug = jnp.concatenate([k_ref[0, pl.ds(k0, tq), :], kx_ref[0, 0, pl.ds(k0, tq), :]], axis=1)
        s = _lane_chunks(lax.dot_general(q_aug, k_aug, _NT, preferred_element_type=_F32))
        if masked:
            s = [jnp.where(lane + c * LANES <= row, sc, NEG_INF) for c, sc in enumerate(s)]
        m_prev = m_ref[...]
        m_new = jnp.maximum(m_prev, jnp.broadcast_to(
            jnp.max(functools.reduce(jnp.maximum, s), axis=1, keepdims=True), m_prev.shape))
        a = jnp.exp2(m_prev - m_new)
        p = [jnp.exp2(sc - m_new) for sc in s]
        l_ref[...] = a * l_ref[...] + functools.reduce(jnp.add, p)
        pv = jnp.dot(jnp.concatenate(p, axis=1).astype(_BF16), v_ref[0, pl.ds(k0, tq), :],
                     preferred_element_type=_F32)
        acc_ref[...] = a * acc_ref[...] + pv
        m_ref[...] = m_new

    def pair(jj, carry):
        step(2 * jj, False)
        step(2 * jj + 1, False)
        return carry

    lax.fori_loop(0, qi // 2, pair, 0)

    @pl.when(qi % 2 == 1)
    def _():
        step(qi - 1, False)

    step(qi, True)
    o_ref[0] = (acc_ref[...] / jnp.sum(l_ref[...], axis=1, keepdims=True)).astype(o_ref.dtype)


def _forgetting_attention(qkv, kx, n_heads):
    b, s, _ = qkv.shape
    tq = _pick(s, (512, 256, 128))
    whole = lambda part: pl.BlockSpec((1, s, HEAD_DIM), lambda bi, h, t: (bi, 0, part * n_heads + h))
    tiled = pl.BlockSpec((1, tq, HEAD_DIM), lambda bi, h, t: (bi, t, h))
    stats = pltpu.VMEM((tq, LANES), _F32)
    return pl.pallas_call(
        functools.partial(_fox_kernel, tq),
        grid=(b, n_heads, s // tq),
        in_specs=[tiled, whole(1), whole(2),
                  pl.BlockSpec((1, 1, s, LANES), lambda bi, h, t: (bi, h, 0, 0))],
        out_specs=tiled,
        out_shape=jax.ShapeDtypeStruct((b, s, n_heads * HEAD_DIM), _BF16),
        scratch_shapes=[stats, stats, pltpu.VMEM((tq, HEAD_DIM), _F32)],
        compiler_params=_params(40, 3),
        name="forgetting_attention",
    )(qkv, qkv, qkv, kx)


def _merge_kernel(alpha, d_model, oa_ref, za_ref, ob_ref, zb_ref, g_ref, x_ref,
                  wa_ref, wb_ref, wo_ref, lng_ref, lnb_ref, o_ref, ob16_ref):
    up_a = jnp.dot(oa_ref[...] * za_ref[...], wa_ref[...], preferred_element_type=_F32)
    up_b = jnp.dot(ob_ref[...] * zb_ref[...], wb_ref[...], preferred_element_type=_F32)
    mix = (g_ref[:, :d_model].astype(_F32) * up_a + g_ref[:, d_model:].astype(_F32) * up_b)
    y = jnp.dot(mix.astype(_BF16), wo_ref[...], preferred_element_type=_F32)
    r = alpha * x_ref[...] + y
    mu = jnp.mean(r, axis=-1, keepdims=True)
    cen = r - mu
    var = jnp.mean(cen * cen, axis=-1, keepdims=True)
    out = cen * lax.rsqrt(var + LN_EPS) * lng_ref[...] + lnb_ref[...]
    o_ref[...] = out
    ob16_ref[...] = out.astype(_BF16)


def _merge(oa, sza, ob, szb, gates, x, w_up_a, w_up_b, w_out, ln_g, ln_b, alpha):
    m, d_model = x.shape
    w = oa.shape[1]
    tm = _pick(m, (256, 128))
    row = lambda n: pl.BlockSpec((tm, n), lambda i: (i, 0))
    const = lambda a, c: pl.BlockSpec((a, c), lambda i: (0, 0), pipeline_mode=pl.Buffered(1))
    return pl.pallas_call(
        functools.partial(_merge_kernel, alpha, d_model),
        grid=(m // tm,),
        in_specs=[row(w), row(w), row(w), row(w), row(2 * d_model), row(d_model),
                  const(w, d_model), const(w, d_model), const(d_model, d_model),
                  const(1, d_model), const(1, d_model)],
        out_specs=[row(d_model), row(d_model)],
        out_shape=[jax.ShapeDtypeStruct((m, d_model), _F32),
                   jax.ShapeDtypeStruct((m, d_model), _BF16)],
        compiler_params=_params(48, 1),
        name="merge_deepnorm",
    )(oa, sza, ob, szb, gates, x, w_up_a, w_up_b, w_out, ln_g, ln_b)


def _rope_tables(seq):
    half = HEAD_DIM // 2
    inv_freq = ROPE_THETA ** (-jnp.arange(half, dtype=_F32) / half)
    ang = jnp.arange(seq, dtype=_F32)[:, None] * inv_freq[None, :]
    cos, sin = jnp.cos(ang), jnp.sin(ang)
    return jnp.concatenate([cos, cos], axis=1), jnp.concatenate([-sin, sin], axis=1)


def _layer(layer, x, xb, tables, w_in, b_forget, b_gate, w_up_a, w_up_b, w_out, ln_g, ln_b, dims):
    batch, seq, d_model, width_a, width_b, alpha = dims
    heads_a, heads_b = width_a // HEAD_DIM, width_b // HEAD_DIM
    m = batch * seq
    q_scale = LOG2E * HEAD_DIM ** -0.5
    o_za, o_qb = 3 * width_a, 4 * width_a
    o_zb, o_f = o_qb + 3 * width_b, o_qb + 4 * width_b
    o_g = o_f + heads_b
    ones = lambda n: jnp.ones((1, n), _F32)
    project = functools.partial(_project, xb=xb, w=w_in, layer=layer, seq=seq)

    scale_a = jnp.concatenate([jnp.full((1, width_a), q_scale, _F32), ones(2 * width_a)], axis=1)
    qkv_a = project("rope", col0=0, n=3 * width_a, extras=(*tables, scale_a), out_dtype=_F32,
                    rope_cols=2 * width_a)
    sz_a = project("silu", col0=o_za, n=width_a, extras=(), out_dtype=_BF16)
    out_a = _dilated_attention(qkv_a.reshape(batch, seq, 3 * width_a), heads_a)

    scale_b = jnp.concatenate([jnp.full((1, width_b), q_scale, _F32), ones(2 * width_b)], axis=1)
    qkv_b = project("linear", col0=o_qb, n=3 * width_b, extras=(scale_b,), out_dtype=_BF16)
    sz_b = project("silu", col0=o_zb, n=width_b, extras=(), out_dtype=_BF16)
    w_f = jnp.pad(w_in[layer, :, o_f:o_g], ((0, 0), (0, LANES - heads_b)))
    b_f = jnp.pad(b_forget, (0, LANES - heads_b))[None, :]
    kx = _forget_bias(xb, w_f, b_f, batch, seq, heads_b)
    out_b = _forgetting_attention(qkv_b.reshape(batch, seq, 3 * width_b), kx, heads_b)

    w_g = w_in[layer, :, o_g:o_g + 2 * d_model][None]
    gates = _project("gate", xb, w_g, 0, 0, 2 * d_model, (b_gate[None, :],), _BF16, seq)
    return _merge(out_a.reshape(m, width_a), sz_a, out_b.reshape(m, width_b), sz_b, gates, x,
                  w_up_a.astype(_BF16), w_up_b.astype(_BF16), w_out.astype(_BF16),
                  ln_g[None, :], ln_b[None, :], alpha)


def kernel(x, w_in, b_forget, b_gate, w_up_a, w_up_b, w_out, ln_g, ln_b):
    batch, seq, d_model = x.shape
    depth = w_in.shape[0]
    width_a, width_b = w_up_a.shape[1], w_up_b.shape[1]
    alpha = float((2 * depth) ** 0.25)
    dims = (batch, seq, d_model, width_a, width_b, alpha)
    tables = _rope_tables(seq)
    x2 = x.reshape(batch * seq, d_model)
    xb = x2.astype(_BF16)
    for l in range(depth):
        x2, xb = _layer(l, x2, xb, tables, w_in, b_forget[l], b_gate[l],
                        w_up_a[l], w_up_b[l], w_out[l], ln_g[l], ln_b[l], dims)
    return x2.reshape(batch, seq, d_model)
```

```python
import functools
import math

import jax
import jax.numpy as jnp
from jax import lax
from jax.experimental import pallas as pl
from jax.experimental.pallas import tpu as pltpu

HEAD_DIM = 128
Q_BLOCK = 128
DILATED_PATTERNS = ((128, 1), (512, 4), (2048, 16))
ROPE_THETA = 10000.0
LN_EPS = 1e-5
LANES = 128
NEG_INF = float("-inf")
LOG2E = math.log2(math.e)
BIAS_PIECES = 3

_F32 = jnp.float32
_BF16 = jnp.bfloat16
_NT = (((1,), (1,)), ((), ()))


def _pick(n, candidates):
    for c in candidates:
        if n % c == 0:
            return c
    raise ValueError(f"no tile in {candidates} divides {n}")


def _params(vmem_mib, n_axes):
    return pltpu.CompilerParams(dimension_semantics=("arbitrary",) * n_axes,
                                vmem_limit_bytes=vmem_mib * 1024 * 1024)


def _lane_chunks(a):
    return [a[:, c * LANES:(c + 1) * LANES] for c in range(a.shape[1] // LANES)]


def _split_bf16(a):
    hi = a.astype(_BF16).astype(_F32)
    mid = (a - hi).astype(_BF16).astype(_F32)
    return hi, mid, a - hi - mid


def _proj_kernel(kind, rope_tiles, x_ref, w_ref, *rest):
    o_ref = rest[-1]
    acc = jnp.dot(x_ref[...], w_ref[0].astype(_BF16), preferred_element_type=_F32)
    if kind == "rope":
        cos_ref, sin_ref, scale_ref = rest[:3]

        @pl.when(pl.program_id(1) < rope_tiles)
        def _():
            cos, sin = cos_ref[...], sin_ref[...]
            for c, t in enumerate(_lane_chunks(acc)):
                r = t * cos + pltpu.roll(t, HEAD_DIM // 2, axis=1) * sin
                o_ref[:, c * LANES:(c + 1) * LANES] = (
                    r * scale_ref[:, c * LANES:(c + 1) * LANES]).astype(o_ref.dtype)

        @pl.when(pl.program_id(1) >= rope_tiles)
        def _():
            o_ref[...] = acc.astype(o_ref.dtype)
    elif kind == "linear":
        o_ref[...] = (acc * rest[0][...]).astype(o_ref.dtype)
    elif kind == "silu":
        o_ref[...] = (acc * jax.nn.sigmoid(acc)).astype(o_ref.dtype)
    elif kind == "gate":
        o_ref[...] = jax.nn.sigmoid(acc + rest[0][...]).astype(o_ref.dtype)
    else:
        raise ValueError(kind)


def _project(kind, xb, w, layer, col0, n, extras, out_dtype, seq, rope_cols=0):
    m, k = xb.shape
    tm = _pick(seq, (1024, 512, 256, 128))
    tn = _pick(math.gcd(math.gcd(n, rope_cols), col0), (512, 256, 128))
    j0 = col0 // tn
    in_specs = [pl.BlockSpec((tm, k), lambda i, j: (i, 0)),
                pl.BlockSpec((1, k, tn), lambda i, j: (layer, 0, j0 + j))]
    if kind == "rope":
        nrow = seq // tm
        table = pl.BlockSpec((tm, HEAD_DIM), lambda i, j: (i % nrow, 0))
        in_specs += [table, table, pl.BlockSpec((1, tn), lambda i, j: (0, j))]
    else:
        in_specs += [pl.BlockSpec((1, tn), lambda i, j: (0, j)) for _ in extras]
    return pl.pallas_call(
        functools.partial(_proj_kernel, kind, rope_cols // tn),
        grid=(m // tm, n // tn),
        in_specs=in_specs,
        out_specs=pl.BlockSpec((tm, tn), lambda i, j: (i, j)),
        out_shape=jax.ShapeDtypeStruct((m, n), out_dtype),
        compiler_params=_params(48, 2),
        name=f"proj_{kind}",
    )(xb, w, *extras)


def _forget_kernel(tiles_per_seq, n_heads, x_ref, w_ref, b_ref, tri_ref, o_ref, carry_ref):
    @pl.when(pl.program_id(0) % tiles_per_seq == 0)
    def _():
        carry_ref[...] = jnp.zeros_like(carry_ref)

    z = jnp.dot(x_ref[...], w_ref[...].astype(_BF16), preferred_element_type=_F32) + b_ref[...]
    log_f = jnp.minimum(z, 0.0) - jnp.log1p(jnp.exp(-jnp.abs(z)))
    tri = tri_ref[...]
    cum = carry_ref[...]
    for piece in _split_bf16(log_f):
        cum = cum + jnp.dot(tri, piece.astype(_BF16), preferred_element_type=_F32)
    carry_ref[...] = cum[-1:, :]
    lane = lax.broadcasted_iota(jnp.int32, cum.shape, 1)
    for h in range(n_heads):
        col = jnp.broadcast_to(cum[:, h:h + 1] * (-LOG2E), cum.shape)
        hi, mid, lo = _split_bf16(col)
        o_ref[0, h] = jnp.where(lane == 0, hi, jnp.where(lane == 1, mid, lo)).astype(_BF16)


def _forget_bias(xb, w_f, b_f, batch, seq, n_heads):
    m, k = xb.shape
    tm = _pick(seq, (512, 256, 128))
    tiles = seq // tm
    tri = (lax.broadcasted_iota(jnp.int32, (tm, tm), 0)
           >= lax.broadcasted_iota(jnp.int32, (tm, tm), 1)).astype(_BF16)
    return pl.pallas_call(
        functools.partial(_forget_kernel, tiles, n_heads),
        grid=(m // tm,),
        in_specs=[pl.BlockSpec((tm, k), lambda i: (i, 0)),
                  pl.BlockSpec((k, LANES), lambda i: (0, 0)),
                  pl.BlockSpec((1, LANES), lambda i: (0, 0)),
                  pl.BlockSpec((tm, tm), lambda i: (0, 0))],
        out_specs=pl.BlockSpec((1, n_heads, tm, LANES), lambda i: (i // tiles, 0, i % tiles, 0)),
        out_shape=jax.ShapeDtypeStruct((batch, n_heads, seq, LANES), _BF16),
        scratch_shapes=[pltpu.VMEM((1, LANES), _F32)],
        compiler_params=_params(32, 1),
        name="forget_bias",
    )(xb, w_f, b_f, tri)


def _dilated_kernel(tile, q_ref, k_ref, v_ref, o_ref, qs_ref, ks_ref, vs_ref, out_ref, lse_ref):
    t = pl.program_id(2)
    slot = t % 2
    row = lax.broadcasted_iota(jnp.int32, (Q_BLOCK, LANES), 0)
    lane = lax.broadcasted_iota(jnp.int32, (Q_BLOCK, LANES), 1)
    in_prev = lane >= row
    in_cur = lane <= row
    ones = jnp.ones((2 * Q_BLOCK, LANES), _BF16)

    @pl.when(t == 0)
    def _():
        vs_ref[1] = jnp.zeros(vs_ref.shape[1:], _BF16)

    for g, (_, d) in enumerate(DILATED_PATTERNS):
        per = tile // d
        for r in range(d):
            src = pl.ds(0, tile) if d == 1 else pl.ds(r, per, stride=d)
            dst = pl.ds(r * per, per)
            qs_ref[g, dst, :] = q_ref[0, src, :].astype(_BF16)
            ks_ref[slot, g, dst, :] = k_ref[0, src, :].astype(_BF16)
            vs_ref[slot, g, dst, :] = v_ref[0, src, :].astype(_BF16)

    for g, (_, d) in enumerate(DILATED_PATTERNS):
        per = tile // d
        for r in range(d):
            for n in range(per // Q_BLOCK):
                base = r * per + n * Q_BLOCK
                q = qs_ref[g, pl.ds(base, Q_BLOCK), :]
                if n > 0:
                    kk = ks_ref[slot, g, pl.ds(base - Q_BLOCK, 2 * Q_BLOCK), :]
                    vv = vs_ref[slot, g, pl.ds(base - Q_BLOCK, 2 * Q_BLOCK), :]
                    prev_ok = in_prev
                else:
                    last = pl.ds((r + 1) * per - Q_BLOCK, Q_BLOCK)
                    kk = jnp.concatenate([ks_ref[1 - slot, g, last, :], ks_ref[slot, g, pl.ds(base, Q_BLOCK), :]], axis=0)
                    vv = jnp.concatenate([vs_ref[1 - slot, g, last, :], vs_ref[slot, g, pl.ds(base, Q_BLOCK), :]], axis=0)
                    prev_ok = in_prev & (t > 0)
                s_prev, s_cur = _lane_chunks(lax.dot_general(q, kk, _NT, preferred_element_type=_F32))
                s_prev = jnp.where(prev_ok, s_prev, NEG_INF)
                s_cur = jnp.where(in_cur, s_cur, NEG_INF)
                m = jnp.broadcast_to(jnp.max(jnp.maximum(s_prev, s_cur), axis=1, keepdims=True), s_cur.shape)
                p = jnp.concatenate([jnp.exp2(s_prev - m), jnp.exp2(s_cur - m)], axis=1).astype(_BF16)
                acc, l = _lane_chunks(jnp.dot(p, jnp.concatenate([vv, ones], axis=1), preferred_element_type=_F32))
                nat = pl.ds(n * Q_BLOCK * d + r, Q_BLOCK) if d == 1 else pl.ds(n * Q_BLOCK * d + r, Q_BLOCK, stride=d)
                out_ref[g, nat, :] = acc / l
                lse_ref[g, nat, :] = m + jnp.log2(l)

    lse_all = jnp.maximum(jnp.maximum(lse_ref[0], lse_ref[1]), lse_ref[2])
    num = jnp.zeros((tile, HEAD_DIM), _F32)
    den = jnp.zeros((tile, LANES), _F32)
    for g in range(len(DILATED_PATTERNS)):
        e = jnp.exp2(lse_ref[g] - lse_all)
        num = num + e * out_ref[g]
        den = den + e
    o_ref[0] = (num / den).astype(o_ref.dtype)


def _dilated_attention(qkv, n_heads):
    b, s, _ = qkv.shape
    tile = Q_BLOCK * DILATED_PATTERNS[-1][1]
    assert s % tile == 0
    n_pat = len(DILATED_PATTERNS)
    tiled = lambda part: pl.BlockSpec((1, tile, HEAD_DIM), lambda bi, h, t: (bi, t, part * n_heads + h))
    stats = pltpu.VMEM((n_pat, tile, LANES), _F32)
    return pl.pallas_call(
        functools.partial(_dilated_kernel, tile),
        grid=(b, n_heads, s // tile),
        in_specs=[tiled(0), tiled(1), tiled(2)],
        out_specs=tiled(0),
        out_shape=jax.ShapeDtypeStruct((b, s, n_heads * HEAD_DIM), _BF16),
        scratch_shapes=[pltpu.VMEM((n_pat, tile, HEAD_DIM), _BF16),
                        pltpu.VMEM((2, n_pat, tile, HEAD_DIM), _BF16),
                        pltpu.VMEM((2, n_pat, tile, HEAD_DIM), _BF16),
                        stats, stats],
        compiler_params=_params(40, 3),
        name="dilated_attention",
    )(qkv, qkv, qkv)


ONES_ROWS = 16


def _fox_kernel(tq, q_ref, k_ref, v_ref, kx_ref, o_ref, vt_ref, m_ref, acc_ref, sa_ref, sb_ref):
    qi = pl.program_id(2)
    seq = v_ref.shape[1]

    @pl.when(qi == 0)
    def _():
        for c in range(seq // tq):
            blk = v_ref[0, c * tq:(c + 1) * tq, :].astype(_F32)
            vt_ref[:HEAD_DIM, c * tq:(c + 1) * tq] = blk.T.astype(_BF16)
        vt_ref[HEAD_DIM:, :] = jnp.ones((ONES_ROWS, seq), _BF16)

    lane = lax.broadcasted_iota(jnp.int32, (tq, LANES), 1)
    q_aug = jnp.concatenate([q_ref[0], (lane < BIAS_PIECES).astype(_BF16)], axis=1)
    m_ref[...] = jnp.full_like(m_ref, NEG_INF)
    acc_ref[...] = jnp.zeros_like(acc_ref)

    def scores(j, s_ref):
        k0 = pl.multiple_of(j * tq, tq)
        k_aug = jnp.concatenate([k_ref[0, pl.ds(k0, tq), :], kx_ref[0, 0, pl.ds(k0, tq), :]], axis=1)
        s_ref[...] = lax.dot_general(k_aug, q_aug, _NT, preferred_element_type=_F32)

    def update(j, s_ref, masked):
        s = s_ref[...]
        if masked:
            key = lax.broadcasted_iota(jnp.int32, s.shape, 0)
            qry = lax.broadcasted_iota(jnp.int32, s.shape, 1)
            s = jnp.where(key <= qry, s, NEG_INF)
        m_prev = m_ref[...]
        m_new = jnp.maximum(m_prev, jnp.max(s, axis=0, keepdims=True))
        a = jnp.exp2(m_prev - m_new)
        p = jnp.exp2(s - m_new).astype(_BF16)
        pv = jnp.dot(vt_ref[:, pl.ds(pl.multiple_of(j * tq, tq), tq)], p,
                     preferred_element_type=_F32)
        acc_ref[...] = a * acc_ref[...] + pv
        m_ref[...] = m_new

    scores(0, sa_ref)

    def pair(jj, carry):
        scores(2 * jj + 1, sb_ref)
        update(2 * jj, sa_ref, False)
        scores(2 * jj + 2, sa_ref)
        update(2 * jj + 1, sb_ref, False)
        return carry

    lax.fori_loop(0, qi // 2, pair, 0)

    @pl.when(qi % 2 == 0)
    def _():
        update(qi, sa_ref, True)

    @pl.when(qi % 2 == 1)
    def _():
        scores(qi, sb_ref)
        update(qi - 1, sa_ref, False)
        update(qi, sb_ref, True)

    out_t = acc_ref[:HEAD_DIM, :] / acc_ref[HEAD_DIM:HEAD_DIM + 1, :]
    o_ref[0] = out_t.T.astype(o_ref.dtype)


def _forgetting_attention(qkv, kx, n_heads):
    b, s, _ = qkv.shape
    tq = _pick(s, (512, 256, 128))
    whole = lambda part: pl.BlockSpec((1, s, HEAD_DIM), lambda bi, h, t: (bi, 0, part * n_heads + h))
    tiled = pl.BlockSpec((1, tq, HEAD_DIM), lambda bi, h, t: (bi, t, h))
    return pl.pallas_call(
        functools.partial(_fox_kernel, tq),
        grid=(b, n_heads, s // tq),
        in_specs=[tiled, whole(1), whole(2),
                  pl.BlockSpec((1, 1, s, LANES), lambda bi, h, t: (bi, h, 0, 0))],
        out_specs=tiled,
        out_shape=jax.ShapeDtypeStruct((b, s, n_heads * HEAD_DIM), _BF16),
        scratch_shapes=[pltpu.VMEM((HEAD_DIM + ONES_ROWS, s), _BF16),
                        pltpu.VMEM((1, tq), _F32),
                        pltpu.VMEM((HEAD_DIM + ONES_ROWS, tq), _F32),
                        pltpu.VMEM((tq, tq), _F32),
                        pltpu.VMEM((tq, tq), _F32)],
        compiler_params=_params(40, 3),
        name="forgetting_attention",
    )(qkv, qkv, qkv, kx)


def _merge_kernel(alpha, d_model, oa_ref, za_ref, ob_ref, zb_ref, g_ref, x_ref,
                  wa_ref, wb_ref, wo_ref, lng_ref, lnb_ref, o_ref, ob16_ref):
    up_a = jnp.dot(oa_ref[...] * za_ref[...], wa_ref[...], preferred_element_type=_F32)
    up_b = jnp.dot(ob_ref[...] * zb_ref[...], wb_ref[...], preferred_element_type=_F32)
    mix = (g_ref[:, :d_model].astype(_F32) * up_a + g_ref[:, d_model:].astype(_F32) * up_b)
    y = jnp.dot(mix.astype(_BF16), wo_ref[...], preferred_element_type=_F32)
    r = alpha * x_ref[...] + y
    mu = jnp.mean(r, axis=-1, keepdims=True)
    cen = r - mu
    var = jnp.mean(cen * cen, axis=-1, keepdims=True)
    out = cen * lax.rsqrt(var + LN_EPS) * lng_ref[...] + lnb_ref[...]
    o_ref[...] = out
    ob16_ref[...] = out.astype(_BF16)


def _merge(oa, sza, ob, szb, gates, x, w_up_a, w_up_b, w_out, ln_g, ln_b, alpha):
    m, d_model = x.shape
    w = oa.shape[1]
    tm = _pick(m, (256, 128))
    row = lambda n: pl.BlockSpec((tm, n), lambda i: (i, 0))
    const = lambda a, c: pl.BlockSpec((a, c), lambda i: (0, 0), pipeline_mode=pl.Buffered(1))
    return pl.pallas_call(
        functools.partial(_merge_kernel, alpha, d_model),
        grid=(m // tm,),
        in_specs=[row(w), row(w), row(w), row(w), row(2 * d_model), row(d_model),
                  const(w, d_model), const(w, d_model), const(d_model, d_model),
                  const(1, d_model), const(1, d_model)],
        out_specs=[row(d_model), row(d_model)],
        out_shape=[jax.ShapeDtypeStruct((m, d_model), _F32),
                   jax.ShapeDtypeStruct((m, d_model), _BF16)],
        compiler_params=_params(48, 1),
        name="merge_deepnorm",
    )(oa, sza, ob, szb, gates, x, w_up_a, w_up_b, w_out, ln_g, ln_b)


def _rope_tables(seq):
    half = HEAD_DIM // 2
    inv_freq = ROPE_THETA ** (-jnp.arange(half, dtype=_F32) / half)
    ang = jnp.arange(seq, dtype=_F32)[:, None] * inv_freq[None, :]
    cos, sin = jnp.cos(ang), jnp.sin(ang)
    return jnp.concatenate([cos, cos], axis=1), jnp.concatenate([-sin, sin], axis=1)


def _layer(layer, x, xb, tables, w_in, b_forget, b_gate, w_up_a, w_up_b, w_out, ln_g, ln_b, dims):
    batch, seq, d_model, width_a, width_b, alpha = dims
    heads_a, heads_b = width_a // HEAD_DIM, width_b // HEAD_DIM
    m = batch * seq
    q_scale = LOG2E * HEAD_DIM ** -0.5
    o_za, o_qb = 3 * width_a, 4 * width_a
    o_zb, o_f = o_qb + 3 * width_b, o_qb + 4 * width_b
    o_g = o_f + heads_b
    ones = lambda n: jnp.ones((1, n), _F32)
    project = functools.partial(_project, xb=xb, w=w_in, layer=layer, seq=seq)

    scale_a = jnp.concatenate([jnp.full((1, width_a), q_scale, _F32), ones(2 * width_a)], axis=1)
    qkv_a = project("rope", col0=0, n=3 * width_a, extras=(*tables, scale_a), out_dtype=_F32,
                    rope_cols=2 * width_a)
    sz_a = project("silu", col0=o_za, n=width_a, extras=(), out_dtype=_BF16)
    out_a = _dilated_attention(qkv_a.reshape(batch, seq, 3 * width_a), heads_a)

    scale_b = jnp.concatenate([jnp.full((1, width_b), q_scale, _F32), ones(2 * width_b)], axis=1)
    qkv_b = project("linear", col0=o_qb, n=3 * width_b, extras=(scale_b,), out_dtype=_BF16)
    sz_b = project("silu", col0=o_zb, n=width_b, extras=(), out_dtype=_BF16)
    w_f = jnp.pad(w_in[layer, :, o_f:o_g], ((0, 0), (0, LANES - heads_b)))
    b_f = jnp.pad(b_forget, (0, LANES - heads_b))[None, :]
    kx = _forget_bias(xb, w_f, b_f, batch, seq, heads_b)
    out_b = _forgetting_attention(qkv_b.reshape(batch, seq, 3 * width_b), kx, heads_b)

    w_g = w_in[layer, :, o_g:o_g + 2 * d_model][None]
    gates = _project("gate", xb, w_g, 0, 0, 2 * d_model, (b_gate[None, :],), _BF16, seq)
    return _merge(out_a.reshape(m, width_a), sz_a, out_b.reshape(m, width_b), sz_b, gates, x,
                  w_up_a.astype(_BF16), w_up_b.astype(_BF16), w_out.astype(_BF16),
                  ln_g[None, :], ln_b[None, :], alpha)


def kernel(x, w_in, b_forget, b_gate, w_up_a, w_up_b, w_out, ln_g, ln_b):
    batch, seq, d_model = x.shape
    depth = w_in.shape[0]
    width_a, width_b = w_up_a.shape[1], w_up_b.shape[1]
    alpha = float((2 * depth) ** 0.25)
    dims = (batch, seq, d_model, width_a, width_b, alpha)
    tables = _rope_tables(seq)
    x2 = x.reshape(batch * seq, d_model)
    xb = x2.astype(_BF16)
    for l in range(depth):
        x2, xb = _layer(l, x2, xb, tables, w_in, b_forget[l], b_gate[l],
                        w_up_a[l], w_up_b[l], w_out[l], ln_g[l], ln_b[l], dims)
    return x2.reshape(batch, seq, d_model)
```

```python
import functools
import math

import jax
import jax.numpy as jnp
from jax import lax
from jax.experimental import pallas as pl
from jax.experimental.pallas import tpu as pltpu

HEAD_DIM = 128
Q_BLOCK = 128
DILATED_PATTERNS = ((128, 1), (512, 4), (2048, 16))
ROPE_THETA = 10000.0
LN_EPS = 1e-5
LANES = 128
NEG_INF = float("-inf")
LOG2E = math.log2(math.e)
BIAS_PIECES = 3

_F32 = jnp.float32
_BF16 = jnp.bfloat16
_NT = (((1,), (1,)), ((), ()))


def _pick(n, candidates):
    for c in candidates:
        if n % c == 0:
            return c
    raise ValueError(f"no tile in {candidates} divides {n}")


def _params(vmem_mib, n_axes):
    return pltpu.CompilerParams(dimension_semantics=("arbitrary",) * n_axes,
                                vmem_limit_bytes=vmem_mib * 1024 * 1024)


def _lane_chunks(a):
    return [a[:, c * LANES:(c + 1) * LANES] for c in range(a.shape[1] // LANES)]


def _split_bf16(a):
    hi = a.astype(_BF16).astype(_F32)
    mid = (a - hi).astype(_BF16).astype(_F32)
    return hi, mid, a - hi - mid


PROJ_SUB_ROWS = 256


def _proj_kernel(kind, rope_tiles, x_ref, w_ref, *rest):
    o_ref = rest[-1]
    w = w_ref[0].astype(_BF16)
    if kind == "rope":
        cos_ref, sin_ref, scale_ref = rest[:3]
        on = jnp.where(pl.program_id(1) < rope_tiles, 1.0, 0.0)
    for rb in range(x_ref.shape[0] // PROJ_SUB_ROWS):
        rows = slice(rb * PROJ_SUB_ROWS, (rb + 1) * PROJ_SUB_ROWS)
        acc = jnp.dot(x_ref[rows, :], w, preferred_element_type=_F32)
        if kind == "rope":
            cos = 1.0 + on * (cos_ref[rows, :] - 1.0)
            sin = on * sin_ref[rows, :]
            for c, t in enumerate(_lane_chunks(acc)):
                r = t * cos + pltpu.roll(t, HEAD_DIM // 2, axis=1) * sin
                o_ref[rows, c * LANES:(c + 1) * LANES] = (
                    r * scale_ref[:, c * LANES:(c + 1) * LANES]).astype(o_ref.dtype)
        elif kind == "linear":
            o_ref[rows, :] = (acc * rest[0][...]).astype(o_ref.dtype)
        elif kind == "silu":
            o_ref[rows, :] = (acc * jax.nn.sigmoid(acc)).astype(o_ref.dtype)
        elif kind == "gate":
            o_ref[rows, :] = jax.nn.sigmoid(acc + rest[0][...]).astype(o_ref.dtype)
        else:
            raise ValueError(kind)


def _project(kind, xb, w, layer, col0, n, extras, out_dtype, seq, rope_cols=0):
    m, k = xb.shape
    tm = _pick(seq, (2048, 1024, 512, 256))
    tn = _pick(math.gcd(math.gcd(n, rope_cols), col0), (512, 256, 128))
    j0 = col0 // tn
    in_specs = [pl.BlockSpec((tm, k), lambda i, j: (i, 0)),
                pl.BlockSpec((1, k, tn), lambda i, j: (layer, 0, j0 + j))]
    if kind == "rope":
        nrow = seq // tm
        table = pl.BlockSpec((tm, HEAD_DIM), lambda i, j: (i % nrow, 0))
        in_specs += [table, table, pl.BlockSpec((1, tn), lambda i, j: (0, j))]
    else:
        in_specs += [pl.BlockSpec((1, tn), lambda i, j: (0, j)) for _ in extras]
    return pl.pallas_call(
        functools.partial(_proj_kernel, kind, rope_cols // tn),
        grid=(m // tm, n // tn),
        in_specs=in_specs,
        out_specs=pl.BlockSpec((tm, tn), lambda i, j: (i, j)),
        out_shape=jax.ShapeDtypeStruct((m, n), out_dtype),
        compiler_params=_params(56, 2),
        name=f"proj_{kind}",
    )(xb, w, *extras)


def _forget_kernel(tiles_per_seq, n_heads, x_ref, w_ref, b_ref, tri_ref, o_ref, carry_ref):
    @pl.when(pl.program_id(0) % tiles_per_seq == 0)
    def _():
        carry_ref[...] = jnp.zeros_like(carry_ref)

    z = jnp.dot(x_ref[...], w_ref[...].astype(_BF16), preferred_element_type=_F32) + b_ref[...]
    log_f = jnp.minimum(z, 0.0) - jnp.log1p(jnp.exp(-jnp.abs(z)))
    tri = tri_ref[...]
    cum = carry_ref[...]
    for piece in _split_bf16(log_f):
        cum = cum + jnp.dot(tri, piece.astype(_BF16), preferred_element_type=_F32)
    carry_ref[...] = cum[-1:, :]
    lane = lax.broadcasted_iota(jnp.int32, cum.shape, 1)
    for h in range(n_heads):
        col = jnp.broadcast_to(cum[:, h:h + 1] * (-LOG2E), cum.shape)
        hi, mid, lo = _split_bf16(col)
        o_ref[0, h] = jnp.where(lane == 0, hi, jnp.where(lane == 1, mid, lo)).astype(_BF16)


def _forget_bias(xb, w_f, b_f, batch, seq, n_heads):
    m, k = xb.shape
    tm = _pick(seq, (512, 256, 128))
    tiles = seq // tm
    tri = (lax.broadcasted_iota(jnp.int32, (tm, tm), 0)
           >= lax.broadcasted_iota(jnp.int32, (tm, tm), 1)).astype(_BF16)
    return pl.pallas_call(
        functools.partial(_forget_kernel, tiles, n_heads),
        grid=(m // tm,),
        in_specs=[pl.BlockSpec((tm, k), lambda i: (i, 0)),
                  pl.BlockSpec((k, LANES), lambda i: (0, 0)),
                  pl.BlockSpec((1, LANES), lambda i: (0, 0)),
                  pl.BlockSpec((tm, tm), lambda i: (0, 0))],
        out_specs=pl.BlockSpec((1, n_heads, tm, LANES), lambda i: (i // tiles, 0, i % tiles, 0)),
        out_shape=jax.ShapeDtypeStruct((batch, n_heads, seq, LANES), _BF16),
        scratch_shapes=[pltpu.VMEM((1, LANES), _F32)],
        compiler_params=_params(32, 1),
        name="forget_bias",
    )(xb, w_f, b_f, tri)


def _dilated_kernel(tile, q_ref, k_ref, v_ref, o_ref, qs_ref, ks_ref, vs_ref, out_ref, lse_ref):
    t = pl.program_id(2)
    slot = t % 2
    row = lax.broadcasted_iota(jnp.int32, (Q_BLOCK, LANES), 0)
    lane = lax.broadcasted_iota(jnp.int32, (Q_BLOCK, LANES), 1)
    in_prev = lane >= row
    in_cur = lane <= row
    ones = jnp.ones((2 * Q_BLOCK, LANES), _BF16)

    @pl.when(t == 0)
    def _():
        vs_ref[1] = jnp.zeros(vs_ref.shape[1:], _BF16)

    for g, (_, d) in enumerate(DILATED_PATTERNS):
        per = tile // d
        for r in range(d):
            src = pl.ds(0, tile) if d == 1 else pl.ds(r, per, stride=d)
            dst = pl.ds(r * per, per)
            qs_ref[g, dst, :] = q_ref[0, src, :].astype(_BF16)
            ks_ref[slot, g, dst, :] = k_ref[0, src, :].astype(_BF16)
            vs_ref[slot, g, dst, :] = v_ref[0, src, :].astype(_BF16)

    for g, (_, d) in enumerate(DILATED_PATTERNS):
        per = tile // d
        for r in range(d):
            for n in range(per // Q_BLOCK):
                base = r * per + n * Q_BLOCK
                q = qs_ref[g, pl.ds(base, Q_BLOCK), :]
                if n > 0:
                    kk = ks_ref[slot, g, pl.ds(base - Q_BLOCK, 2 * Q_BLOCK), :]
                    vv = vs_ref[slot, g, pl.ds(base - Q_BLOCK, 2 * Q_BLOCK), :]
                    prev_ok = in_prev
                else:
                    last = pl.ds((r + 1) * per - Q_BLOCK, Q_BLOCK)
                    kk = jnp.concatenate([ks_ref[1 - slot, g, last, :], ks_ref[slot, g, pl.ds(base, Q_BLOCK), :]], axis=0)
                    vv = jnp.concatenate([vs_ref[1 - slot, g, last, :], vs_ref[slot, g, pl.ds(base, Q_BLOCK), :]], axis=0)
                    prev_ok = in_prev & (t > 0)
                s_prev, s_cur = _lane_chunks(lax.dot_general(q, kk, _NT, preferred_element_type=_F32))
                s_prev = jnp.where(prev_ok, s_prev, NEG_INF)
                s_cur = jnp.where(in_cur, s_cur, NEG_INF)
                m = jnp.broadcast_to(jnp.max(jnp.maximum(s_prev, s_cur), axis=1, keepdims=True), s_cur.shape)
                p = jnp.concatenate([jnp.exp2(s_prev - m), jnp.exp2(s_cur - m)], axis=1).astype(_BF16)
                acc, l = _lane_chunks(jnp.dot(p, jnp.concatenate([vv, ones], axis=1), preferred_element_type=_F32))
                nat = pl.ds(n * Q_BLOCK * d + r, Q_BLOCK) if d == 1 else pl.ds(n * Q_BLOCK * d + r, Q_BLOCK, stride=d)
                out_ref[g, nat, :] = acc / l
                lse_ref[g, nat, :] = m + jnp.log2(l)

    lse_all = jnp.maximum(jnp.maximum(lse_ref[0], lse_ref[1]), lse_ref[2])
    num = jnp.zeros((tile, HEAD_DIM), _F32)
    den = jnp.zeros((tile, LANES), _F32)
    for g in range(len(DILATED_PATTERNS)):
        e = jnp.exp2(lse_ref[g] - lse_all)
        num = num + e * out_ref[g]
        den = den + e
    o_ref[0] = (num / den).astype(o_ref.dtype)


def _dilated_attention(qkv, n_heads):
    b, s, _ = qkv.shape
    tile = Q_BLOCK * DILATED_PATTERNS[-1][1]
    assert s % tile == 0
    n_pat = len(DILATED_PATTERNS)
    tiled = lambda part: pl.BlockSpec((1, tile, HEAD_DIM), lambda bi, h, t: (bi, t, part * n_heads + h))
    stats = pltpu.VMEM((n_pat, tile, LANES), _F32)
    return pl.pallas_call(
        functools.partial(_dilated_kernel, tile),
        grid=(b, n_heads, s // tile),
        in_specs=[tiled(0), tiled(1), tiled(2)],
        out_specs=tiled(0),
        out_shape=jax.ShapeDtypeStruct((b, s, n_heads * HEAD_DIM), _BF16),
        scratch_shapes=[pltpu.VMEM((n_pat, tile, HEAD_DIM), _BF16),
                        pltpu.VMEM((2, n_pat, tile, HEAD_DIM), _BF16),
                        pltpu.VMEM((2, n_pat, tile, HEAD_DIM), _BF16),
                        stats, stats],
        compiler_params=_params(40, 3),
        name="dilated_attention",
    )(qkv, qkv, qkv)


ONES_ROWS = 16


def _fox_kernel(tq, q_ref, k_ref, v_ref, kx_ref, o_ref, vt_ref, m_ref, acc_ref, sa_ref, sb_ref):
    qi = pl.program_id(2)
    seq = v_ref.shape[1]
    tk = tq // 2

    @pl.when(qi == 0)
    def _():
        for c in range(seq // tk):
            blk = v_ref[0, c * tk:(c + 1) * tk, :].astype(_F32)
            vt_ref[:HEAD_DIM, c * tk:(c + 1) * tk] = blk.T.astype(_BF16)
        vt_ref[HEAD_DIM:, :] = jnp.ones((ONES_ROWS, seq), _BF16)

    lane = lax.broadcasted_iota(jnp.int32, (tq, LANES), 1)
    q_aug = jnp.concatenate([q_ref[0], (lane < BIAS_PIECES).astype(_BF16)], axis=1)
    m_ref[...] = jnp.full_like(m_ref, NEG_INF)
    acc_ref[...] = jnp.zeros_like(acc_ref)

    def scores(j, s_ref):
        k0 = pl.multiple_of(j * tk, tk)
        k_aug = jnp.concatenate([k_ref[0, pl.ds(k0, tk), :], kx_ref[0, 0, pl.ds(k0, tk), :]], axis=1)
        s_ref[...] = lax.dot_general(k_aug, q_aug, _NT, preferred_element_type=_F32)

    def update(j, s_ref, diagonal_half=None):
        s = s_ref[...]
        if diagonal_half is not None:
            key = lax.broadcasted_iota(jnp.int32, s.shape, 0) + diagonal_half * tk
            qry = lax.broadcasted_iota(jnp.int32, s.shape, 1)
            s = jnp.where(key <= qry, s, NEG_INF)
        m_prev = m_ref[...]
        m_new = jnp.maximum(m_prev, jnp.max(s, axis=0, keepdims=True))
        a = jnp.exp2(m_prev - m_new)
        p = jnp.exp2(s - m_new).astype(_BF16)
        pv = jnp.dot(vt_ref[:, pl.ds(pl.multiple_of(j * tk, tk), tk)], p,
                     preferred_element_type=_F32)
        acc_ref[...] = a * acc_ref[...] + pv
        m_ref[...] = m_new

    scores(0, sa_ref)

    def pair(jj, carry):
        scores(2 * jj + 1, sb_ref)
        update(2 * jj, sa_ref)
        scores(2 * jj + 2, sa_ref)
        update(2 * jj + 1, sb_ref)
        return carry

    lax.fori_loop(0, qi, pair, 0)
    scores(2 * qi + 1, sb_ref)
    update(2 * qi, sa_ref, diagonal_half=0)
    update(2 * qi + 1, sb_ref, diagonal_half=1)

    out_t = acc_ref[:HEAD_DIM, :] / acc_ref[HEAD_DIM:HEAD_DIM + 1, :]
    o_ref[0] = out_t.T.astype(o_ref.dtype)


def _forgetting_attention(qkv, kx, n_heads):
    b, s, _ = qkv.shape
    tq = _pick(s, (1024, 512, 256))
    whole = lambda part: pl.BlockSpec((1, s, HEAD_DIM), lambda bi, h, t: (bi, 0, part * n_heads + h))
    tiled = pl.BlockSpec((1, tq, HEAD_DIM), lambda bi, h, t: (bi, t, h))
    return pl.pallas_call(
        functools.partial(_fox_kernel, tq),
        grid=(b, n_heads, s // tq),
        in_specs=[tiled, whole(1), whole(2),
                  pl.BlockSpec((1, 1, s, LANES), lambda bi, h, t: (bi, h, 0, 0))],
        out_specs=tiled,
        out_shape=jax.ShapeDtypeStruct((b, s, n_heads * HEAD_DIM), _BF16),
        scratch_shapes=[pltpu.VMEM((HEAD_DIM + ONES_ROWS, s), _BF16),
                        pltpu.VMEM((1, tq), _F32),
                        pltpu.VMEM((HEAD_DIM + ONES_ROWS, tq), _F32),
                        pltpu.VMEM((tq // 2, tq), _F32),
                        pltpu.VMEM((tq // 2, tq), _F32)],
        compiler_params=_params(48, 3),
        name="forgetting_attention",
    )(qkv, qkv, qkv, kx)


def _merge_kernel(alpha, d_model, oa_ref, za_ref, ob_ref, zb_ref, g_ref, x_ref,
                  wa_ref, wb_ref, wo_ref, lng_ref, lnb_ref, o_ref, ob16_ref):
    up_a = jnp.dot(oa_ref[...] * za_ref[...], wa_ref[...], preferred_element_type=_F32)
    up_b = jnp.dot(ob_ref[...] * zb_ref[...], wb_ref[...], preferred_element_type=_F32)
    mix = (g_ref[:, :d_model].astype(_F32) * up_a + g_ref[:, d_model:].astype(_F32) * up_b)
    y = jnp.dot(mix.astype(_BF16), wo_ref[...], preferred_element_type=_F32)
    r = alpha * x_ref[...] + y
    mu = jnp.mean(r, axis=-1, keepdims=True)
    cen = r - mu
    var = jnp.mean(cen * cen, axis=-1, keepdims=True)
    out = cen * lax.rsqrt(var + LN_EPS) * lng_ref[...] + lnb_ref[...]
    o_ref[...] = out
    ob16_ref[...] = out.astype(_BF16)


def _merge(oa, sza, ob, szb, gates, x, w_up_a, w_up_b, w_out, ln_g, ln_b, alpha):
    m, d_model = x.shape
    w = oa.shape[1]
    tm = _pick(m, (256, 128))
    row = lambda n: pl.BlockSpec((tm, n), lambda i: (i, 0))
    const = lambda a, c: pl.BlockSpec((a, c), lambda i: (0, 0), pipeline_mode=pl.Buffered(1))
    return pl.pallas_call(
        functools.partial(_merge_kernel, alpha, d_model),
        grid=(m // tm,),
        in_specs=[row(w), row(w), row(w), row(w), row(2 * d_model), row(d_model),
                  const(w, d_model), const(w, d_model), const(d_model, d_model),
                  const(1, d_model), const(1, d_model)],
        out_specs=[row(d_model), row(d_model)],
        out_shape=[jax.ShapeDtypeStruct((m, d_model), _F32),
                   jax.ShapeDtypeStruct((m, d_model), _BF16)],
        compiler_params=_params(48, 1),
        name="merge_deepnorm",
    )(oa, sza, ob, szb, gates, x, w_up_a, w_up_b, w_out, ln_g, ln_b)


def _rope_tables(seq):
    half = HEAD_DIM // 2
    inv_freq = ROPE_THETA ** (-jnp.arange(half, dtype=_F32) / half)
    ang = jnp.arange(seq, dtype=_F32)[:, None] * inv_freq[None, :]
    cos, sin = jnp.cos(ang), jnp.sin(ang)
    return jnp.concatenate([cos, cos], axis=1), jnp.concatenate([-sin, sin], axis=1)


def _layer(layer, x, xb, tables, w_in, b_forget, b_gate, w_up_a, w_up_b, w_out, ln_g, ln_b, dims):
    batch, seq, d_model, width_a, width_b, alpha = dims
    heads_a, heads_b = width_a // HEAD_DIM, width_b // HEAD_DIM
    m = batch * seq
    q_scale = LOG2E * HEAD_DIM ** -0.5
    o_za, o_qb = 3 * width_a, 4 * width_a
    o_zb, o_f = o_qb + 3 * width_b, o_qb + 4 * width_b
    o_g = o_f + heads_b
    ones = lambda n: jnp.ones((1, n), _F32)
    project = functools.partial(_project, xb=xb, w=w_in, layer=layer, seq=seq)

    scale_a = jnp.concatenate([jnp.full((1, width_a), q_scale, _F32), ones(2 * width_a)], axis=1)
    qkv_a = project("rope", col0=0, n=3 * width_a, extras=(*tables, scale_a), out_dtype=_F32,
                    rope_cols=2 * width_a)
    sz_a = project("silu", col0=o_za, n=width_a, extras=(), out_dtype=_BF16)
    out_a = _dilated_attention(qkv_a.reshape(batch, seq, 3 * width_a), heads_a)

    scale_b = jnp.concatenate([jnp.full((1, width_b), q_scale, _F32), ones(2 * width_b)], axis=1)
    qkv_b = project("linear", col0=o_qb, n=3 * width_b, extras=(scale_b,), out_dtype=_BF16)
    sz_b = project("silu", col0=o_zb, n=width_b, extras=(), out_dtype=_BF16)
    w_f = jnp.pad(w_in[layer, :, o_f:o_g], ((0, 0), (0, LANES - heads_b)))
    b_f = jnp.pad(b_forget, (0, LANES - heads_b))[None, :]
    kx = _forget_bias(xb, w_f, b_f, batch, seq, heads_b)
    out_b = _forgetting_attention(qkv_b.reshape(batch, seq, 3 * width_b), kx, heads_b)

    w_g = w_in[layer, :, o_g:o_g + 2 * d_model][None]
    gates = _project("gate", xb, w_g, 0, 0, 2 * d_model, (b_gate[None, :],), _BF16, seq)
    return _merge(out_a.reshape(m, width_a), sz_a, out_b.reshape(m, width_b), sz_b, gates, x,
                  w_up_a.astype(_BF16), w_up_b.astype(_BF16), w_out.astype(_BF16),
                  ln_g[None, :], ln_b[None, :], alpha)


def kernel(x, w_in, b_forget, b_gate, w_up_a, w_up_b, w_out, ln_g, ln_b):
    batch, seq, d_model = x.shape
    depth = w_in.shape[0]
    width_a, width_b = w_up_a.shape[1], w_up_b.shape[1]
    alpha = float((2 * depth) ** 0.25)
    dims = (batch, seq, d_model, width_a, width_b, alpha)
    tables = _rope_tables(seq)
    x2 = x.reshape(batch * seq, d_model)
    xb = x2.astype(_BF16)
    for l in range(depth):
        x2, xb = _layer(l, x2, xb, tables, w_in, b_forget[l], b_gate[l],
                        w_up_a[l], w_up_b[l], w_out[l], ln_g[l], ln_b[l], dims)
    return x2.reshape(batch, seq, d_model)
```

```python
import functools
import math

import jax
import jax.numpy as jnp
from jax import lax
from jax.experimental import pallas as pl
from jax.experimental.pallas import tpu as pltpu

HEAD_DIM = 128
Q_BLOCK = 128
DILATED_PATTERNS = ((128, 1), (512, 4), (2048, 16))
ROPE_THETA = 10000.0
LN_EPS = 1e-5
LANES = 128
NEG_INF = float("-inf")
LOG2E = math.log2(math.e)
BIAS_PIECES = 3

_F32 = jnp.float32
_BF16 = jnp.bfloat16
_NT = (((1,), (1,)), ((), ()))


def _pick(n, candidates):
    for c in candidates:
        if n % c == 0:
            return c
    raise ValueError(f"no tile in {candidates} divides {n}")


def _params(vmem_mib, n_axes):
    return pltpu.CompilerParams(dimension_semantics=("arbitrary",) * n_axes,
                                vmem_limit_bytes=vmem_mib * 1024 * 1024)


def _lane_chunks(a):
    return [a[:, c * LANES:(c + 1) * LANES] for c in range(a.shape[1] // LANES)]


def _split_bf16(a):
    hi = a.astype(_BF16).astype(_F32)
    mid = (a - hi).astype(_BF16).astype(_F32)
    return hi, mid, a - hi - mid


PROJ_SUB_ROWS = 256


def _proj_kernel(kind, rope_tiles, x_ref, w_ref, *rest):
    o_ref = rest[-1]
    w = w_ref[0].astype(_BF16)
    if kind == "rope":
        cos_ref, sin_ref, scale_ref = rest[:3]
        on = jnp.where(pl.program_id(1) < rope_tiles, 1.0, 0.0)
    for rb in range(x_ref.shape[0] // PROJ_SUB_ROWS):
        rows = slice(rb * PROJ_SUB_ROWS, (rb + 1) * PROJ_SUB_ROWS)
        acc = jnp.dot(x_ref[rows, :], w, preferred_element_type=_F32)
        if kind == "rope":
            cos = 1.0 + on * (cos_ref[rows, :] - 1.0)
            sin = on * sin_ref[rows, :]
            for c, t in enumerate(_lane_chunks(acc)):
                r = t * cos + pltpu.roll(t, HEAD_DIM // 2, axis=1) * sin
                o_ref[rows, c * LANES:(c + 1) * LANES] = (
                    r * scale_ref[:, c * LANES:(c + 1) * LANES]).astype(o_ref.dtype)
        elif kind == "linear":
            o_ref[rows, :] = (acc * rest[0][...]).astype(o_ref.dtype)
        elif kind == "silu":
            o_ref[rows, :] = (acc * jax.nn.sigmoid(acc)).astype(o_ref.dtype)
        elif kind == "gate":
            o_ref[rows, :] = jax.nn.sigmoid(acc + rest[0][...]).astype(o_ref.dtype)
        else:
            raise ValueError(kind)


def _project(kind, xb, w, layer, col0, n, extras, out_dtype, seq, rope_cols=0):
    m, k = xb.shape
    tm = _pick(seq, (2048, 1024, 512, 256))
    tn = _pick(math.gcd(math.gcd(n, rope_cols), col0), (512, 256, 128))
    j0 = col0 // tn
    in_specs = [pl.BlockSpec((tm, k), lambda i, j: (i, 0)),
                pl.BlockSpec((1, k, tn), lambda i, j: (layer, 0, j0 + j))]
    if kind == "rope":
        nrow = seq // tm
        table = pl.BlockSpec((tm, HEAD_DIM), lambda i, j: (i % nrow, 0))
        in_specs += [table, table, pl.BlockSpec((1, tn), lambda i, j: (0, j))]
    else:
        in_specs += [pl.BlockSpec((1, tn), lambda i, j: (0, j)) for _ in extras]
    return pl.pallas_call(
        functools.partial(_proj_kernel, kind, rope_cols // tn),
        grid=(m // tm, n // tn),
        in_specs=in_specs,
        out_specs=pl.BlockSpec((tm, tn), lambda i, j: (i, j)),
        out_shape=jax.ShapeDtypeStruct((m, n), out_dtype),
        compiler_params=_params(56, 2),
        name=f"proj_{kind}",
    )(xb, w, *extras)


def _forget_kernel(tiles_per_seq, n_heads, x_ref, w_ref, b_ref, tri_ref, o_ref, carry_ref):
    @pl.when(pl.program_id(0) % tiles_per_seq == 0)
    def _():
        carry_ref[...] = jnp.zeros_like(carry_ref)

    z = jnp.dot(x_ref[...], w_ref[...].astype(_BF16), preferred_element_type=_F32) + b_ref[...]
    log_f = jnp.minimum(z, 0.0) - jnp.log1p(jnp.exp(-jnp.abs(z)))
    tri = tri_ref[...]
    cum = carry_ref[...]
    for piece in _split_bf16(log_f):
        cum = cum + jnp.dot(tri, piece.astype(_BF16), preferred_element_type=_F32)
    carry_ref[...] = cum[-1:, :]
    head = lax.broadcasted_iota(jnp.int32, (LANES, LANES), 0)
    lane = lax.broadcasted_iota(jnp.int32, (LANES, LANES), 1)
    out = jnp.zeros(cum.shape, _F32)
    for k, piece in enumerate(_split_bf16(cum * (-LOG2E))):
        place = ((lane == BIAS_PIECES * head + k) & (head < n_heads)).astype(_BF16)
        out = out + jnp.dot(piece.astype(_BF16), place, preferred_element_type=_F32)
    o_ref[...] = out.astype(_BF16)


def _forget_bias(xb, w_f, b_f, batch, seq, n_heads):
    assert BIAS_PIECES * n_heads <= LANES
    m, k = xb.shape
    tm = _pick(seq, (512, 256, 128))
    tiles = seq // tm
    tri = (lax.broadcasted_iota(jnp.int32, (tm, tm), 0)
           >= lax.broadcasted_iota(jnp.int32, (tm, tm), 1)).astype(_BF16)
    return pl.pallas_call(
        functools.partial(_forget_kernel, tiles, n_heads),
        grid=(m // tm,),
        in_specs=[pl.BlockSpec((tm, k), lambda i: (i, 0)),
                  pl.BlockSpec((k, LANES), lambda i: (0, 0)),
                  pl.BlockSpec((1, LANES), lambda i: (0, 0)),
                  pl.BlockSpec((tm, tm), lambda i: (0, 0))],
        out_specs=pl.BlockSpec((tm, LANES), lambda i: (i, 0)),
        out_shape=jax.ShapeDtypeStruct((m, LANES), _BF16),
        scratch_shapes=[pltpu.VMEM((1, LANES), _F32)],
        compiler_params=_params(32, 1),
        name="forget_bias",
    )(xb, w_f, b_f, tri)


def _dilated_kernel(tile, q_ref, k_ref, v_ref, o_ref, qs_ref, ks_ref, vs_ref, out_ref, lse_ref):
    t = pl.program_id(2)
    slot = t % 2
    row = lax.broadcasted_iota(jnp.int32, (Q_BLOCK, LANES), 0)
    lane = lax.broadcasted_iota(jnp.int32, (Q_BLOCK, LANES), 1)
    in_prev = lane >= row
    in_cur = lane <= row
    ones = jnp.ones((2 * Q_BLOCK, LANES), _BF16)

    @pl.when(t == 0)
    def _():
        vs_ref[1] = jnp.zeros(vs_ref.shape[1:], _BF16)

    for g, (_, d) in enumerate(DILATED_PATTERNS):
        per = tile // d
        for r in range(d):
            src = pl.ds(0, tile) if d == 1 else pl.ds(r, per, stride=d)
            dst = pl.ds(r * per, per)
            qs_ref[g, dst, :] = q_ref[0, src, :].astype(_BF16)
            ks_ref[slot, g, dst, :] = k_ref[0, src, :].astype(_BF16)
            vs_ref[slot, g, dst, :] = v_ref[0, src, :].astype(_BF16)

    for g, (_, d) in enumerate(DILATED_PATTERNS):
        per = tile // d
        for r in range(d):
            for n in range(per // Q_BLOCK):
                base = r * per + n * Q_BLOCK
                q = qs_ref[g, pl.ds(base, Q_BLOCK), :]
                if n > 0:
                    kk = ks_ref[slot, g, pl.ds(base - Q_BLOCK, 2 * Q_BLOCK), :]
                    vv = vs_ref[slot, g, pl.ds(base - Q_BLOCK, 2 * Q_BLOCK), :]
                    prev_ok = in_prev
                else:
                    last = pl.ds((r + 1) * per - Q_BLOCK, Q_BLOCK)
                    kk = jnp.concatenate([ks_ref[1 - slot, g, last, :], ks_ref[slot, g, pl.ds(base, Q_BLOCK), :]], axis=0)
                    vv = jnp.concatenate([vs_ref[1 - slot, g, last, :], vs_ref[slot, g, pl.ds(base, Q_BLOCK), :]], axis=0)
                    prev_ok = in_prev & (t > 0)
                s_prev, s_cur = _lane_chunks(lax.dot_general(q, kk, _NT, preferred_element_type=_F32))
                s_prev = jnp.where(prev_ok, s_prev, NEG_INF)
                s_cur = jnp.where(in_cur, s_cur, NEG_INF)
                m = jnp.broadcast_to(jnp.max(jnp.maximum(s_prev, s_cur), axis=1, keepdims=True), s_cur.shape)
                p = jnp.concatenate([jnp.exp2(s_prev - m), jnp.exp2(s_cur - m)], axis=1).astype(_BF16)
                acc, l = _lane_chunks(jnp.dot(p, jnp.concatenate([vv, ones], axis=1), preferred_element_type=_F32))
                nat = pl.ds(n * Q_BLOCK * d + r, Q_BLOCK) if d == 1 else pl.ds(n * Q_BLOCK * d + r, Q_BLOCK, stride=d)
                out_ref[g, nat, :] = acc / l
                lse_ref[g, nat, :] = m + jnp.log2(l)

    lse_all = jnp.maximum(jnp.maximum(lse_ref[0], lse_ref[1]), lse_ref[2])
    num = jnp.zeros((tile, HEAD_DIM), _F32)
    den = jnp.zeros((tile, LANES), _F32)
    for g in range(len(DILATED_PATTERNS)):
        e = jnp.exp2(lse_ref[g] - lse_all)
        num = num + e * out_ref[g]
        den = den + e
    o_ref[0] = (num / den).astype(o_ref.dtype)


def _dilated_attention(qkv, n_heads):
    b, s, _ = qkv.shape
    tile = Q_BLOCK * DILATED_PATTERNS[-1][1]
    assert s % tile == 0
    n_pat = len(DILATED_PATTERNS)
    tiled = lambda part: pl.BlockSpec((1, tile, HEAD_DIM), lambda bi, h, t: (bi, t, part * n_heads + h))
    stats = pltpu.VMEM((n_pat, tile, LANES), _F32)
    return pl.pallas_call(
        functools.partial(_dilated_kernel, tile),
        grid=(b, n_heads, s // tile),
        in_specs=[tiled(0), tiled(1), tiled(2)],
        out_specs=tiled(0),
        out_shape=jax.ShapeDtypeStruct((b, s, n_heads * HEAD_DIM), _BF16),
        scratch_shapes=[pltpu.VMEM((n_pat, tile, HEAD_DIM), _BF16),
                        pltpu.VMEM((2, n_pat, tile, HEAD_DIM), _BF16),
                        pltpu.VMEM((2, n_pat, tile, HEAD_DIM), _BF16),
                        stats, stats],
        compiler_params=_params(40, 3),
        name="dilated_attention",
    )(qkv, qkv, qkv)


ONES_ROWS = 16


def _fox_kernel(tq, q_ref, k_ref, v_ref, kx_ref, o_ref, vt_ref, m_ref, acc_ref, sa_ref, sb_ref):
    qi = pl.program_id(2)
    seq = v_ref.shape[1]
    tk = tq // 2

    @pl.when(qi == 0)
    def _():
        for c in range(seq // tk):
            blk = v_ref[0, c * tk:(c + 1) * tk, :].astype(_F32)
            vt_ref[:HEAD_DIM, c * tk:(c + 1) * tk] = blk.T.astype(_BF16)
        vt_ref[HEAD_DIM:, :] = jnp.ones((ONES_ROWS, seq), _BF16)

    lane = lax.broadcasted_iota(jnp.int32, (tq, LANES), 1)
    first = BIAS_PIECES * pl.program_id(1)
    ones = ((lane >= first) & (lane < first + BIAS_PIECES)).astype(_BF16)
    q_aug = jnp.concatenate([q_ref[0], ones], axis=1)
    m_ref[...] = jnp.full_like(m_ref, NEG_INF)
    acc_ref[...] = jnp.zeros_like(acc_ref)

    def scores(j, s_ref, q0=0):
        k0 = pl.multiple_of(j * tk, tk)
        k_aug = jnp.concatenate([k_ref[0, pl.ds(k0, tk), :], kx_ref[0, pl.ds(k0, tk), :]], axis=1)
        s_ref[:, :tq - q0] = lax.dot_general(k_aug, q_aug[q0:], _NT, preferred_element_type=_F32)

    def update(j, s_ref, q0=0, diagonal=False):
        s = s_ref[:, :tq - q0]
        if diagonal:
            key = lax.broadcasted_iota(jnp.int32, s.shape, 0)
            qry = lax.broadcasted_iota(jnp.int32, s.shape, 1)
            s = jnp.where(key <= qry, s, NEG_INF)
        m_prev = m_ref[:, q0:]
        m_new = jnp.maximum(m_prev, jnp.max(s, axis=0, keepdims=True))
        a = jnp.exp2(m_prev - m_new)
        p = jnp.exp2(s - m_new).astype(_BF16)
        pv = jnp.dot(vt_ref[:, pl.ds(pl.multiple_of(j * tk, tk), tk)], p,
                     preferred_element_type=_F32)
        acc_ref[:, q0:] = a * acc_ref[:, q0:] + pv
        m_ref[:, q0:] = m_new

    def pair(j0):
        scores(j0 + 1, sb_ref)
        update(j0, sa_ref)
        scores(j0 + 2, sa_ref)
        update(j0 + 1, sb_ref)

    def two_pairs(jj, carry):
        pair(4 * jj)
        pair(4 * jj + 2)
        return carry

    scores(0, sa_ref)
    lax.fori_loop(0, qi // 2, two_pairs, 0)

    @pl.when(qi % 2 == 1)
    def _():
        pair(2 * qi - 2)

    scores(2 * qi + 1, sb_ref, q0=tk)
    update(2 * qi, sa_ref, diagonal=True)
    update(2 * qi + 1, sb_ref, q0=tk, diagonal=True)

    out_t = acc_ref[:HEAD_DIM, :] / acc_ref[HEAD_DIM:HEAD_DIM + 1, :]
    o_ref[0] = out_t.T.astype(o_ref.dtype)


def _forgetting_attention(qkv, kx, n_heads):
    b, s, _ = qkv.shape
    tq = _pick(s, (1024, 512, 256))
    whole = lambda part: pl.BlockSpec((1, s, HEAD_DIM), lambda bi, h, t: (bi, 0, part * n_heads + h))
    tiled = pl.BlockSpec((1, tq, HEAD_DIM), lambda bi, h, t: (bi, t, h))
    return pl.pallas_call(
        functools.partial(_fox_kernel, tq),
        grid=(b, n_heads, s // tq),
        in_specs=[tiled, whole(1), whole(2),
                  pl.BlockSpec((1, s, LANES), lambda bi, h, t: (bi, 0, 0))],
        out_specs=tiled,
        out_shape=jax.ShapeDtypeStruct((b, s, n_heads * HEAD_DIM), _BF16),
        scratch_shapes=[pltpu.VMEM((HEAD_DIM + ONES_ROWS, s), _BF16),
                        pltpu.VMEM((1, tq), _F32),
                        pltpu.VMEM((HEAD_DIM + ONES_ROWS, tq), _F32),
                        pltpu.VMEM((tq // 2, tq), _F32),
                        pltpu.VMEM((tq // 2, tq), _F32)],
        compiler_params=_params(48, 3),
        name="forgetting_attention",
    )(qkv, qkv, qkv, kx)


def _merge_kernel(alpha, d_model, oa_ref, za_ref, ob_ref, zb_ref, g_ref, x_ref,
                  wa_ref, wb_ref, wo_ref, lng_ref, lnb_ref, o_ref, ob16_ref):
    up_a = jnp.dot(oa_ref[...] * za_ref[...], wa_ref[...], preferred_element_type=_F32)
    up_b = jnp.dot(ob_ref[...] * zb_ref[...], wb_ref[...], preferred_element_type=_F32)
    mix = (g_ref[:, :d_model].astype(_F32) * up_a + g_ref[:, d_model:].astype(_F32) * up_b)
    y = jnp.dot(mix.astype(_BF16), wo_ref[...], preferred_element_type=_F32)
    r = alpha * x_ref[...] + y
    mu = jnp.mean(r, axis=-1, keepdims=True)
    cen = r - mu
    var = jnp.mean(cen * cen, axis=-1, keepdims=True)
    out = cen * lax.rsqrt(var + LN_EPS) * lng_ref[...] + lnb_ref[...]
    o_ref[...] = out
    ob16_ref[...] = out.astype(_BF16)


def _merge(oa, sza, ob, szb, gates, x, w_up_a, w_up_b, w_out, ln_g, ln_b, alpha):
    m, d_model = x.shape
    w = oa.shape[1]
    tm = _pick(m, (256, 128))
    row = lambda n: pl.BlockSpec((tm, n), lambda i: (i, 0))
    const = lambda a, c: pl.BlockSpec((a, c), lambda i: (0, 0), pipeline_mode=pl.Buffered(1))
    return pl.pallas_call(
        functools.partial(_merge_kernel, alpha, d_model),
        grid=(m // tm,),
        in_specs=[row(w), row(w), row(w), row(w), row(2 * d_model), row(d_model),
                  const(w, d_model), const(w, d_model), const(d_model, d_model),
                  const(1, d_model), const(1, d_model)],
        out_specs=[row(d_model), row(d_model)],
        out_shape=[jax.ShapeDtypeStruct((m, d_model), _F32),
                   jax.ShapeDtypeStruct((m, d_model), _BF16)],
        compiler_params=_params(48, 1),
        name="merge_deepnorm",
    )(oa, sza, ob, szb, gates, x, w_up_a, w_up_b, w_out, ln_g, ln_b)


def _rope_tables(seq):
    half = HEAD_DIM // 2
    inv_freq = ROPE_THETA ** (-jnp.arange(half, dtype=_F32) / half)
    ang = jnp.arange(seq, dtype=_F32)[:, None] * inv_freq[None, :]
    cos, sin = jnp.cos(ang), jnp.sin(ang)
    return jnp.concatenate([cos, cos], axis=1), jnp.concatenate([-sin, sin], axis=1)


def _layer(layer, x, xb, tables, w_in, b_forget, b_gate, w_up_a, w_up_b, w_out, ln_g, ln_b, dims):
    batch, seq, d_model, width_a, width_b, alpha = dims
    heads_a, heads_b = width_a // HEAD_DIM, width_b // HEAD_DIM
    m = batch * seq
    q_scale = LOG2E * HEAD_DIM ** -0.5
    o_za, o_qb = 3 * width_a, 4 * width_a
    o_zb, o_f = o_qb + 3 * width_b, o_qb + 4 * width_b
    o_g = o_f + heads_b
    ones = lambda n: jnp.ones((1, n), _F32)
    project = functools.partial(_project, xb=xb, w=w_in, layer=layer, seq=seq)

    scale_a = jnp.concatenate([jnp.full((1, width_a), q_scale, _F32), ones(2 * width_a)], axis=1)
    qkv_a = project("rope", col0=0, n=3 * width_a, extras=(*tables, scale_a), out_dtype=_F32,
                    rope_cols=2 * width_a)
    sz_a = project("silu", col0=o_za, n=width_a, extras=(), out_dtype=_BF16)
    out_a = _dilated_attention(qkv_a.reshape(batch, seq, 3 * width_a), heads_a)

    scale_b = jnp.concatenate([jnp.full((1, width_b), q_scale, _F32), ones(2 * width_b)], axis=1)
    qkv_b = project("linear", col0=o_qb, n=3 * width_b, extras=(scale_b,), out_dtype=_BF16)
    sz_b = project("silu", col0=o_zb, n=width_b, extras=(), out_dtype=_BF16)
    w_f = jnp.pad(w_in[layer, :, o_f:o_g], ((0, 0), (0, LANES - heads_b)))
    b_f = jnp.pad(b_forget, (0, LANES - heads_b))[None, :]
    kx = _forget_bias(xb, w_f, b_f, batch, seq, heads_b)
    out_b = _forgetting_attention(qkv_b.reshape(batch, seq, 3 * width_b), kx.reshape(batch, seq, LANES), heads_b)

    w_g = w_in[layer, :, o_g:o_g + 2 * d_model][None]
    gates = _project("gate", xb, w_g, 0, 0, 2 * d_model, (b_gate[None, :],), _BF16, seq)
    return _merge(out_a.reshape(m, width_a), sz_a, out_b.reshape(m, width_b), sz_b, gates, x,
                  w_up_a.astype(_BF16), w_up_b.astype(_BF16), w_out.astype(_BF16),
                  ln_g[None, :], ln_b[None, :], alpha)


def kernel(x, w_in, b_forget, b_gate, w_up_a, w_up_b, w_out, ln_g, ln_b):
    batch, seq, d_model = x.shape
    depth = w_in.shape[0]
    width_a, width_b = w_up_a.shape[1], w_up_b.shape[1]
    alpha = float((2 * depth) ** 0.25)
    dims = (batch, seq, d_model, width_a, width_b, alpha)
    tables = _rope_tables(seq)
    x2 = x.reshape(batch * seq, d_model)
    xb = x2.astype(_BF16)
    for l in range(depth):
        x2, xb = _layer(l, x2, xb, tables, w_in, b_forget[l], b_gate[l],
                        w_up_a[l], w_up_b[l], w_out[l], ln_g[l], ln_b[l], dims)
    return x2.reshape(batch, seq, d_model)
```

```python
import functools
import math

import jax
import jax.numpy as jnp
from jax import lax
from jax.experimental import pallas as pl
from jax.experimental.pallas import tpu as pltpu

HEAD_DIM = 128
Q_BLOCK = 128
DILATED_PATTERNS = ((128, 1), (512, 4), (2048, 16))
ROPE_THETA = 10000.0
LN_EPS = 1e-5
LANES = 128
NEG_INF = float("-inf")
LOG2E = math.log2(math.e)
BIAS_PIECES = 3

_F32 = jnp.float32
_BF16 = jnp.bfloat16
_NT = (((1,), (1,)), ((), ()))


def _pick(n, candidates):
    for c in candidates:
        if n % c == 0:
            return c
    raise ValueError(f"no tile in {candidates} divides {n}")


def _params(vmem_mib, n_axes):
    return pltpu.CompilerParams(dimension_semantics=("arbitrary",) * n_axes,
                                vmem_limit_bytes=vmem_mib * 1024 * 1024)


def _lane_chunks(a):
    return [a[:, c * LANES:(c + 1) * LANES] for c in range(a.shape[1] // LANES)]


def _split_bf16(a):
    hi = a.astype(_BF16).astype(_F32)
    mid = (a - hi).astype(_BF16).astype(_F32)
    return hi, mid, a - hi - mid


PROJ_SUB_ROWS = 256


def _proj_kernel(kind, rope_tiles, x_ref, w_ref, *rest):
    o_ref = rest[-1]
    w = w_ref[0].astype(_BF16)
    if kind == "rope":
        cos_ref, sin_ref, scale_ref = rest[:3]
        on = jnp.where(pl.program_id(1) < rope_tiles, 1.0, 0.0)
    for rb in range(x_ref.shape[0] // PROJ_SUB_ROWS):
        rows = slice(rb * PROJ_SUB_ROWS, (rb + 1) * PROJ_SUB_ROWS)
        acc = jnp.dot(x_ref[rows, :], w, preferred_element_type=_F32)
        if kind == "rope":
            cos = 1.0 + on * (cos_ref[rows, :] - 1.0)
            sin = on * sin_ref[rows, :]
            for c, t in enumerate(_lane_chunks(acc)):
                r = t * cos + pltpu.roll(t, HEAD_DIM // 2, axis=1) * sin
                o_ref[rows, c * LANES:(c + 1) * LANES] = (
                    r * scale_ref[:, c * LANES:(c + 1) * LANES]).astype(o_ref.dtype)
        elif kind == "linear":
            o_ref[rows, :] = (acc * rest[0][...]).astype(o_ref.dtype)
        elif kind == "silu":
            o_ref[rows, :] = (acc * jax.nn.sigmoid(acc)).astype(o_ref.dtype)
        elif kind == "gate":
            o_ref[rows, :] = jax.nn.sigmoid(acc + rest[0][...]).astype(o_ref.dtype)
        else:
            raise ValueError(kind)


def _project(kind, xb, w, layer, col0, n, extras, out_dtype, seq, rope_cols=0):
    m, k = xb.shape
    tm = _pick(seq, (2048, 1024, 512, 256))
    tn = _pick(math.gcd(math.gcd(n, rope_cols), col0), (1024, 768, 512, 256, 128))
    j0 = col0 // tn
    in_specs = [pl.BlockSpec((tm, k), lambda i, j: (i, 0)),
                pl.BlockSpec((1, k, tn), lambda i, j: (layer, 0, j0 + j))]
    if kind == "rope":
        nrow = seq // tm
        table = pl.BlockSpec((tm, HEAD_DIM), lambda i, j: (i % nrow, 0))
        in_specs += [table, table, pl.BlockSpec((1, tn), lambda i, j: (0, j))]
    else:
        in_specs += [pl.BlockSpec((1, tn), lambda i, j: (0, j)) for _ in extras]
    return pl.pallas_call(
        functools.partial(_proj_kernel, kind, rope_cols // tn),
        grid=(m // tm, n // tn),
        in_specs=in_specs,
        out_specs=pl.BlockSpec((tm, tn), lambda i, j: (i, j)),
        out_shape=jax.ShapeDtypeStruct((m, n), out_dtype),
        compiler_params=_params(56, 2),
        name=f"proj_{kind}",
    )(xb, w, *extras)


def _forget_kernel(tiles_per_seq, n_heads, x_ref, w_ref, b_ref, tri_ref, o_ref, carry_ref):
    @pl.when(pl.program_id(0) % tiles_per_seq == 0)
    def _():
        carry_ref[...] = jnp.zeros_like(carry_ref)

    z = jnp.dot(x_ref[...], w_ref[...].astype(_BF16), preferred_element_type=_F32) + b_ref[...]
    log_f = jnp.minimum(z, 0.0) - jnp.log1p(jnp.exp(-jnp.abs(z)))
    tri = tri_ref[...]
    cum = carry_ref[...]
    for piece in _split_bf16(log_f):
        cum = cum + jnp.dot(tri, piece.astype(_BF16), preferred_element_type=_F32)
    carry_ref[...] = cum[-1:, :]
    head = lax.broadcasted_iota(jnp.int32, (LANES, LANES), 0)
    lane = lax.broadcasted_iota(jnp.int32, (LANES, LANES), 1)
    out = jnp.zeros(cum.shape, _F32)
    for k, piece in enumerate(_split_bf16(cum * (-LOG2E))):
        place = ((lane == BIAS_PIECES * head + k) & (head < n_heads)).astype(_BF16)
        out = out + jnp.dot(piece.astype(_BF16), place, preferred_element_type=_F32)
    o_ref[...] = out.astype(_BF16)


def _forget_bias(xb, w_f, b_f, batch, seq, n_heads):
    assert BIAS_PIECES * n_heads <= LANES
    m, k = xb.shape
    tm = _pick(seq, (512, 256, 128))
    tiles = seq // tm
    tri = (lax.broadcasted_iota(jnp.int32, (tm, tm), 0)
           >= lax.broadcasted_iota(jnp.int32, (tm, tm), 1)).astype(_BF16)
    return pl.pallas_call(
        functools.partial(_forget_kernel, tiles, n_heads),
        grid=(m // tm,),
        in_specs=[pl.BlockSpec((tm, k), lambda i: (i, 0)),
                  pl.BlockSpec((k, LANES), lambda i: (0, 0)),
                  pl.BlockSpec((1, LANES), lambda i: (0, 0)),
                  pl.BlockSpec((tm, tm), lambda i: (0, 0))],
        out_specs=pl.BlockSpec((tm, LANES), lambda i: (i, 0)),
        out_shape=jax.ShapeDtypeStruct((m, LANES), _BF16),
        scratch_shapes=[pltpu.VMEM((1, LANES), _F32)],
        compiler_params=_params(32, 1),
        name="forget_bias",
    )(xb, w_f, b_f, tri)


def _dilated_kernel(tile, q_ref, k_ref, v_ref, o_ref, qs_ref, ks_ref, vs_ref, out_ref, lse_ref, mid_ref):
    t = pl.program_id(2)
    slot = t % 2
    row = lax.broadcasted_iota(jnp.int32, (Q_BLOCK, LANES), 0)
    lane = lax.broadcasted_iota(jnp.int32, (Q_BLOCK, LANES), 1)
    in_prev = lane >= row
    in_cur = lane <= row
    ones = jnp.ones((2 * Q_BLOCK, LANES), _BF16)

    @pl.when(t == 0)
    def _():
        vs_ref[1] = jnp.zeros(vs_ref.shape[1:], _BF16)

    (_, d1), (_, d2), (_, d3) = DILATED_PATTERNS
    assert d1 == 1 and d3 % d2 == 0
    per2, per3, step3 = tile // d2, tile // d3, d3 // d2
    for src_ref, dst_ref in ((q_ref, qs_ref), (k_ref, ks_ref.at[slot]), (v_ref, vs_ref.at[slot])):
        dst_ref[0] = src_ref[0].astype(_BF16)
        for r2 in range(d2):
            cls = src_ref[0, pl.ds(r2, per2, stride=d2), :]
            mid_ref[pl.ds(r2 * per2, per2), :] = cls
            dst_ref[1, pl.ds(r2 * per2, per2), :] = cls.astype(_BF16)
        for r3 in range(d3):
            cls = mid_ref[pl.ds((r3 % d2) * per2 + r3 // d2, per3, stride=step3), :]
            dst_ref[2, pl.ds(r3 * per3, per3), :] = cls.astype(_BF16)

    for g, (_, d) in enumerate(DILATED_PATTERNS):
        per = tile // d
        for r in range(d):
            for n in range(per // Q_BLOCK):
                base = r * per + n * Q_BLOCK
                q = qs_ref[g, pl.ds(base, Q_BLOCK), :]
                if n > 0:
                    kk = ks_ref[slot, g, pl.ds(base - Q_BLOCK, 2 * Q_BLOCK), :]
                    vv = vs_ref[slot, g, pl.ds(base - Q_BLOCK, 2 * Q_BLOCK), :]
                    prev_ok = in_prev
                else:
                    last = pl.ds((r + 1) * per - Q_BLOCK, Q_BLOCK)
                    kk = jnp.concatenate([ks_ref[1 - slot, g, last, :], ks_ref[slot, g, pl.ds(base, Q_BLOCK), :]], axis=0)
                    vv = jnp.concatenate([vs_ref[1 - slot, g, last, :], vs_ref[slot, g, pl.ds(base, Q_BLOCK), :]], axis=0)
                    prev_ok = in_prev & (t > 0)
                s_prev, s_cur = _lane_chunks(lax.dot_general(q, kk, _NT, preferred_element_type=_F32))
                s_prev = jnp.where(prev_ok, s_prev, NEG_INF)
                s_cur = jnp.where(in_cur, s_cur, NEG_INF)
                m = jnp.broadcast_to(jnp.max(jnp.maximum(s_prev, s_cur), axis=1, keepdims=True), s_cur.shape)
                p = jnp.concatenate([jnp.exp2(s_prev - m), jnp.exp2(s_cur - m)], axis=1).astype(_BF16)
                acc, l = _lane_chunks(jnp.dot(p, jnp.concatenate([vv, ones], axis=1), preferred_element_type=_F32))
                nat = pl.ds(n * Q_BLOCK * d + r, Q_BLOCK) if d == 1 else pl.ds(n * Q_BLOCK * d + r, Q_BLOCK, stride=d)
                out_ref[g, nat, :] = acc / l
                lse_ref[g, nat, :] = m + jnp.log2(l)

    lse_all = jnp.maximum(jnp.maximum(lse_ref[0], lse_ref[1]), lse_ref[2])
    num = jnp.zeros((tile, HEAD_DIM), _F32)
    den = jnp.zeros((tile, LANES), _F32)
    for g in range(len(DILATED_PATTERNS)):
        e = jnp.exp2(lse_ref[g] - lse_all)
        num = num + e * out_ref[g]
        den = den + e
    o_ref[0] = (num / den).astype(o_ref.dtype)


def _dilated_attention(qkv, n_heads):
    b, s, _ = qkv.shape
    tile = Q_BLOCK * DILATED_PATTERNS[-1][1]
    assert s % tile == 0
    n_pat = len(DILATED_PATTERNS)
    tiled = lambda part: pl.BlockSpec((1, tile, HEAD_DIM), lambda bi, h, t: (bi, t, part * n_heads + h))
    stats = pltpu.VMEM((n_pat, tile, LANES), _F32)
    return pl.pallas_call(
        functools.partial(_dilated_kernel, tile),
        grid=(b, n_heads, s // tile),
        in_specs=[tiled(0), tiled(1), tiled(2)],
        out_specs=tiled(0),
        out_shape=jax.ShapeDtypeStruct((b, s, n_heads * HEAD_DIM), _BF16),
        scratch_shapes=[pltpu.VMEM((n_pat, tile, HEAD_DIM), _BF16),
                        pltpu.VMEM((2, n_pat, tile, HEAD_DIM), _BF16),
                        pltpu.VMEM((2, n_pat, tile, HEAD_DIM), _BF16),
                        stats, stats, pltpu.VMEM((tile, HEAD_DIM), _F32)],
        compiler_params=_params(40, 3),
        name="dilated_attention",
    )(qkv, qkv, qkv)


ONES_ROWS = 16


def _fox_kernel(tq, q_ref, k_ref, v_ref, kx_ref, o_ref, vt_ref, m_ref, acc_ref, sa_ref, sb_ref):
    qi = pl.program_id(2)
    seq = v_ref.shape[1]
    tk = tq // 2

    @pl.when(qi == 0)
    def _():
        for c in range(seq // tk):
            blk = v_ref[0, c * tk:(c + 1) * tk, :].astype(_F32)
            vt_ref[:HEAD_DIM, c * tk:(c + 1) * tk] = blk.T.astype(_BF16)
        vt_ref[HEAD_DIM:, :] = jnp.ones((ONES_ROWS, seq), _BF16)

    depth = lax.broadcasted_iota(jnp.int32, (LANES, tq), 0)
    first = BIAS_PIECES * pl.program_id(1)
    ones = ((depth >= first) & (depth < first + BIAS_PIECES)).astype(_BF16)
    q_aug = jnp.concatenate([q_ref[0].astype(_F32).T.astype(_BF16), ones], axis=0)
    m_ref[...] = jnp.full_like(m_ref, NEG_INF)
    acc_ref[...] = jnp.zeros_like(acc_ref)

    def scores(j, s_ref, q0=0):
        k0 = pl.multiple_of(j * tk, tk)
        k_aug = jnp.concatenate([k_ref[0, pl.ds(k0, tk), :], kx_ref[0, pl.ds(k0, tk), :]], axis=1)
        s_ref[:, :tq - q0] = jnp.dot(k_aug, q_aug[:, q0:], preferred_element_type=_F32)

    def update(j, s_ref, q0=0, diagonal=False):
        s = s_ref[:, :tq - q0]
        if diagonal:
            key = lax.broadcasted_iota(jnp.int32, s.shape, 0)
            qry = lax.broadcasted_iota(jnp.int32, s.shape, 1)
            s = jnp.where(key <= qry, s, NEG_INF)
        m_prev = m_ref[:, q0:]
        m_new = jnp.maximum(m_prev, jnp.max(s, axis=0, keepdims=True))
        a = jnp.exp2(m_prev - m_new)
        p = jnp.exp2(s - m_new).astype(_BF16)
        pv = jnp.dot(vt_ref[:, pl.ds(pl.multiple_of(j * tk, tk), tk)], p,
                     preferred_element_type=_F32)
        acc_ref[:, q0:] = a * acc_ref[:, q0:] + pv
        m_ref[:, q0:] = m_new

    def pair(j0):
        scores(j0 + 1, sb_ref)
        update(j0, sa_ref)
        scores(j0 + 2, sa_ref)
        update(j0 + 1, sb_ref)

    def two_pairs(jj, carry):
        pair(4 * jj)
        pair(4 * jj + 2)
        return carry

    scores(0, sa_ref)
    lax.fori_loop(0, qi // 2, two_pairs, 0)

    @pl.when(qi % 2 == 1)
    def _():
        pair(2 * qi - 2)

    scores(2 * qi + 1, sb_ref, q0=tk)
    update(2 * qi, sa_ref, diagonal=True)
    update(2 * qi + 1, sb_ref, q0=tk, diagonal=True)

    out_t = acc_ref[:HEAD_DIM, :] / acc_ref[HEAD_DIM:HEAD_DIM + 1, :]
    o_ref[0] = out_t.T.astype(o_ref.dtype)


def _forgetting_attention(qkv, kx, n_heads):
    b, s, _ = qkv.shape
    tq = _pick(s, (1024, 512, 256))
    whole = lambda part: pl.BlockSpec((1, s, HEAD_DIM), lambda bi, h, t: (bi, 0, part * n_heads + h))
    tiled = pl.BlockSpec((1, tq, HEAD_DIM), lambda bi, h, t: (bi, t, h))
    return pl.pallas_call(
        functools.partial(_fox_kernel, tq),
        grid=(b, n_heads, s // tq),
        in_specs=[tiled, whole(1), whole(2),
                  pl.BlockSpec((1, s, LANES), lambda bi, h, t: (bi, 0, 0))],
        out_specs=tiled,
        out_shape=jax.ShapeDtypeStruct((b, s, n_heads * HEAD_DIM), _BF16),
        scratch_shapes=[pltpu.VMEM((HEAD_DIM + ONES_ROWS, s), _BF16),
                        pltpu.VMEM((1, tq), _F32),
                        pltpu.VMEM((HEAD_DIM + ONES_ROWS, tq), _F32),
                        pltpu.VMEM((tq // 2, tq), _F32),
                        pltpu.VMEM((tq // 2, tq), _F32)],
        compiler_params=_params(48, 3),
        name="forgetting_attention",
    )(qkv, qkv, qkv, kx)


def _merge_kernel(alpha, d_model, oa_ref, za_ref, ob_ref, zb_ref, g_ref, x_ref,
                  wa_ref, wb_ref, wo_ref, lng_ref, lnb_ref, o_ref, ob16_ref):
    up_a = jnp.dot(oa_ref[...] * za_ref[...], wa_ref[...], preferred_element_type=_F32)
    up_b = jnp.dot(ob_ref[...] * zb_ref[...], wb_ref[...], preferred_element_type=_F32)
    mix = (g_ref[:, :d_model].astype(_F32) * up_a + g_ref[:, d_model:].astype(_F32) * up_b)
    y = jnp.dot(mix.astype(_BF16), wo_ref[...], preferred_element_type=_F32)
    r = alpha * x_ref[...] + y
    mu = jnp.mean(r, axis=-1, keepdims=True)
    cen = r - mu
    var = jnp.mean(cen * cen, axis=-1, keepdims=True)
    out = cen * lax.rsqrt(var + LN_EPS) * lng_ref[...] + lnb_ref[...]
    o_ref[...] = out
    ob16_ref[...] = out.astype(_BF16)


def _merge(oa, sza, ob, szb, gates, x, w_up_a, w_up_b, w_out, ln_g, ln_b, alpha):
    m, d_model = x.shape
    w = oa.shape[1]
    tm = _pick(m, (256, 128))
    row = lambda n: pl.BlockSpec((tm, n), lambda i: (i, 0))
    const = lambda a, c: pl.BlockSpec((a, c), lambda i: (0, 0), pipeline_mode=pl.Buffered(1))
    return pl.pallas_call(
        functools.partial(_merge_kernel, alpha, d_model),
        grid=(m // tm,),
        in_specs=[row(w), row(w), row(w), row(w), row(2 * d_model), row(d_model),
                  const(w, d_model), const(w, d_model), const(d_model, d_model),
                  const(1, d_model), const(1, d_model)],
        out_specs=[row(d_model), row(d_model)],
        out_shape=[jax.ShapeDtypeStruct((m, d_model), _F32),
                   jax.ShapeDtypeStruct((m, d_model), _BF16)],
        compiler_params=_params(48, 1),
        name="merge_deepnorm",
    )(oa, sza, ob, szb, gates, x, w_up_a, w_up_b, w_out, ln_g, ln_b)


def _rope_tables(seq):
    half = HEAD_DIM // 2
    inv_freq = ROPE_THETA ** (-jnp.arange(half, dtype=_F32) / half)
    ang = jnp.arange(seq, dtype=_F32)[:, None] * inv_freq[None, :]
    cos, sin = jnp.cos(ang), jnp.sin(ang)
    return jnp.concatenate([cos, cos], axis=1), jnp.concatenate([-sin, sin], axis=1)


def _layer(layer, x, xb, tables, w_in, b_forget, b_gate, w_up_a, w_up_b, w_out, ln_g, ln_b, dims):
    batch, seq, d_model, width_a, width_b, alpha = dims
    heads_a, heads_b = width_a // HEAD_DIM, width_b // HEAD_DIM
    m = batch * seq
    q_scale = LOG2E * HEAD_DIM ** -0.5
    o_za, o_qb = 3 * width_a, 4 * width_a
    o_zb, o_f = o_qb + 3 * width_b, o_qb + 4 * width_b
    o_g = o_f + heads_b
    ones = lambda n: jnp.ones((1, n), _F32)
    project = functools.partial(_project, xb=xb, w=w_in, layer=layer, seq=seq)

    scale_a = jnp.concatenate([jnp.full((1, width_a), q_scale, _F32), ones(2 * width_a)], axis=1)
    qkv_a = project("rope", col0=0, n=3 * width_a, extras=(*tables, scale_a), out_dtype=_F32,
                    rope_cols=2 * width_a)
    sz_a = project("silu", col0=o_za, n=width_a, extras=(), out_dtype=_BF16)
    out_a = _dilated_attention(qkv_a.reshape(batch, seq, 3 * width_a), heads_a)

    scale_b = jnp.concatenate([jnp.full((1, width_b), q_scale, _F32), ones(2 * width_b)], axis=1)
    qkv_b = project("linear", col0=o_qb, n=3 * width_b, extras=(scale_b,), out_dtype=_BF16)
    sz_b = project("silu", col0=o_zb, n=width_b, extras=(), out_dtype=_BF16)
    w_f = jnp.pad(w_in[layer, :, o_f:o_g], ((0, 0), (0, LANES - heads_b)))
    b_f = jnp.pad(b_forget, (0, LANES - heads_b))[None, :]
    kx = _forget_bias(xb, w_f, b_f, batch, seq, heads_b)
    out_b = _forgetting_attention(qkv_b.reshape(batch, seq, 3 * width_b), kx.reshape(batch, seq, LANES), heads_b)

    w_g = w_in[layer, :, o_g:o_g + 2 * d_model][None]
    gates = _project("gate", xb, w_g, 0, 0, 2 * d_model, (b_gate[None, :],), _BF16, seq)
    return _merge(out_a.reshape(m, width_a), sz_a, out_b.reshape(m, width_b), sz_b, gates, x,
                  w_up_a.astype(_BF16), w_up_b.astype(_BF16), w_out.astype(_BF16),
                  ln_g[None, :], ln_b[None, :], alpha)


def kernel(x, w_in, b_forget, b_gate, w_up_a, w_up_b, w_out, ln_g, ln_b):
    batch, seq, d_model = x.shape
    depth = w_in.shape[0]
    width_a, width_b = w_up_a.shape[1], w_up_b.shape[1]
    alpha = float((2 * depth) ** 0.25)
    dims = (batch, seq, d_model, width_a, width_b, alpha)
    tables = _rope_tables(seq)
    x2 = x.reshape(batch * seq, d_model)
    xb = x2.astype(_BF16)
    for l in range(depth):
        x2, xb = _layer(l, x2, xb, tables, w_in, b_forget[l], b_gate[l],
                        w_up_a[l], w_up_b[l], w_out[l], ln_g[l], ln_b[l], dims)
    return x2.reshape(batch, seq, d_model)
```

```python
import functools
import math

import jax
import jax.numpy as jnp
from jax import lax
from jax.experimental import pallas as pl
from jax.experimental.pallas import tpu as pltpu

HEAD_DIM = 128
Q_BLOCK = 128
DILATED_PATTERNS = ((128, 1), (512, 4), (2048, 16))
ROPE_THETA = 10000.0
LN_EPS = 1e-5
LANES = 128
NEG_INF = float("-inf")
LOG2E = math.log2(math.e)
BIAS_PIECES = 3

_F32 = jnp.float32
_BF16 = jnp.bfloat16
_NT = (((1,), (1,)), ((), ()))


def _pick(n, candidates):
    for c in candidates:
        if n % c == 0:
            return c
    raise ValueError(f"no tile in {candidates} divides {n}")


def _params(vmem_mib, n_axes):
    return pltpu.CompilerParams(dimension_semantics=("arbitrary",) * n_axes,
                                vmem_limit_bytes=vmem_mib * 1024 * 1024)


def _lane_chunks(a):
    return [a[:, c * LANES:(c + 1) * LANES] for c in range(a.shape[1] // LANES)]


def _split_bf16(a):
    hi = a.astype(_BF16).astype(_F32)
    mid = (a - hi).astype(_BF16).astype(_F32)
    return hi, mid, a - hi - mid


PROJ_SUB_ROWS = 256


def _proj_kernel(kind, rope_tiles, x_ref, w_ref, *rest):
    o_ref = rest[-1]
    w = w_ref[0].astype(_BF16)
    if kind == "rope":
        cos_ref, sin_ref, scale_ref = rest[:3]
        on = jnp.where(pl.program_id(1) < rope_tiles, 1.0, 0.0)
    def put_heads(rows, c, val):
        o_ref[c, rows, :] = val.astype(o_ref.dtype)

    for rb in range(x_ref.shape[0] // PROJ_SUB_ROWS):
        rows = slice(rb * PROJ_SUB_ROWS, (rb + 1) * PROJ_SUB_ROWS)
        acc = jnp.dot(x_ref[rows, :], w, preferred_element_type=_F32)
        if kind == "rope":
            cos = 1.0 + on * (cos_ref[rows, :] - 1.0)
            sin = on * sin_ref[rows, :]
            for c, t in enumerate(_lane_chunks(acc)):
                r = t * cos + pltpu.roll(t, HEAD_DIM // 2, axis=1) * sin
                put_heads(rows, c, r * scale_ref[:, c * LANES:(c + 1) * LANES])
        elif kind == "linear":
            for c, t in enumerate(_lane_chunks(acc * rest[0][...])):
                put_heads(rows, c, t)
        elif kind == "silu":
            o_ref[rows, :] = (acc * jax.nn.sigmoid(acc)).astype(o_ref.dtype)
        elif kind == "gate":
            o_ref[rows, :] = jax.nn.sigmoid(acc + rest[0][...]).astype(o_ref.dtype)
        else:
            raise ValueError(kind)


def _project(kind, xb, w, layer, col0, n, extras, out_dtype, seq, rope_cols=0):
    m, k = xb.shape
    tm = _pick(seq, (2048, 1024, 512, 256))
    tn = _pick(math.gcd(math.gcd(n, rope_cols), col0), (1024, 768, 512, 256, 128))
    j0 = col0 // tn
    nrow = seq // tm
    if kind in ("rope", "linear"):
        out_spec = pl.BlockSpec((None, tn // LANES, tm, LANES), lambda i, j: (i // nrow, j, i % nrow, 0))
        out_shape = jax.ShapeDtypeStruct((m // seq, n // LANES, seq, LANES), out_dtype)
    else:
        out_spec = pl.BlockSpec((tm, tn), lambda i, j: (i, j))
        out_shape = jax.ShapeDtypeStruct((m, n), out_dtype)
    in_specs = [pl.BlockSpec((tm, k), lambda i, j: (i, 0)),
                pl.BlockSpec((1, k, tn), lambda i, j: (layer, 0, j0 + j))]
    if kind == "rope":
        table = pl.BlockSpec((tm, HEAD_DIM), lambda i, j: (i % nrow, 0))
        in_specs += [table, table, pl.BlockSpec((1, tn), lambda i, j: (0, j))]
    else:
        in_specs += [pl.BlockSpec((1, tn), lambda i, j: (0, j)) for _ in extras]
    return pl.pallas_call(
        functools.partial(_proj_kernel, kind, rope_cols // tn),
        grid=(m // tm, n // tn),
        in_specs=in_specs,
        out_specs=out_spec,
        out_shape=out_shape,
        compiler_params=_params(56, 2),
        name=f"proj_{kind}",
    )(xb, w, *extras)


def _forget_kernel(tiles_per_seq, n_heads, x_ref, w_ref, b_ref, tri_ref, o_ref, carry_ref):
    @pl.when(pl.program_id(0) % tiles_per_seq == 0)
    def _():
        carry_ref[...] = jnp.zeros_like(carry_ref)

    z = jnp.dot(x_ref[...], w_ref[...].astype(_BF16), preferred_element_type=_F32) + b_ref[...]
    log_f = jnp.minimum(z, 0.0) - jnp.log1p(jnp.exp(-jnp.abs(z)))
    tri = tri_ref[...]
    cum = carry_ref[...]
    for piece in _split_bf16(log_f):
        cum = cum + jnp.dot(tri, piece.astype(_BF16), preferred_element_type=_F32)
    carry_ref[...] = cum[-1:, :]
    head = lax.broadcasted_iota(jnp.int32, (LANES, LANES), 0)
    lane = lax.broadcasted_iota(jnp.int32, (LANES, LANES), 1)
    out = jnp.zeros(cum.shape, _F32)
    for k, piece in enumerate(_split_bf16(cum * (-LOG2E))):
        place = ((lane == BIAS_PIECES * head + k) & (head < n_heads)).astype(_BF16)
        out = out + jnp.dot(piece.astype(_BF16), place, preferred_element_type=_F32)
    o_ref[...] = out.astype(_BF16)


def _forget_bias(xb, w_f, b_f, batch, seq, n_heads):
    assert BIAS_PIECES * n_heads <= LANES
    m, k = xb.shape
    tm = _pick(seq, (512, 256, 128))
    tiles = seq // tm
    tri = (lax.broadcasted_iota(jnp.int32, (tm, tm), 0)
           >= lax.broadcasted_iota(jnp.int32, (tm, tm), 1)).astype(_BF16)
    return pl.pallas_call(
        functools.partial(_forget_kernel, tiles, n_heads),
        grid=(m // tm,),
        in_specs=[pl.BlockSpec((tm, k), lambda i: (i, 0)),
                  pl.BlockSpec((k, LANES), lambda i: (0, 0)),
                  pl.BlockSpec((1, LANES), lambda i: (0, 0)),
                  pl.BlockSpec((tm, tm), lambda i: (0, 0))],
        out_specs=pl.BlockSpec((tm, LANES), lambda i: (i, 0)),
        out_shape=jax.ShapeDtypeStruct((m, LANES), _BF16),
        scratch_shapes=[pltpu.VMEM((1, LANES), _F32)],
        compiler_params=_params(32, 1),
        name="forget_bias",
    )(xb, w_f, b_f, tri)


def _dilated_kernel(tile, q_ref, k_ref, v_ref, o_ref, qs_ref, ks_ref, vs_ref, out_ref, lse_ref, mid_ref):
    t = pl.program_id(2)
    slot = t % 2
    row = lax.broadcasted_iota(jnp.int32, (Q_BLOCK, LANES), 0)
    lane = lax.broadcasted_iota(jnp.int32, (Q_BLOCK, LANES), 1)
    in_prev = lane >= row
    in_cur = lane <= row
    ones = jnp.ones((2 * Q_BLOCK, LANES), _BF16)

    @pl.when(t == 0)
    def _():
        vs_ref[1] = jnp.zeros(vs_ref.shape[1:], _BF16)

    (_, d1), (_, d2), (_, d3) = DILATED_PATTERNS
    assert d1 == 1 and d3 % d2 == 0
    per2, per3, step3 = tile // d2, tile // d3, d3 // d2
    for src_ref, dst_ref in ((q_ref, qs_ref), (k_ref, ks_ref.at[slot]), (v_ref, vs_ref.at[slot])):
        dst_ref[0] = src_ref[0].astype(_BF16)
        for r2 in range(d2):
            cls = src_ref[0, pl.ds(r2, per2, stride=d2), :]
            mid_ref[pl.ds(r2 * per2, per2), :] = cls
            dst_ref[1, pl.ds(r2 * per2, per2), :] = cls.astype(_BF16)
        for r3 in range(d3):
            cls = mid_ref[pl.ds((r3 % d2) * per2 + r3 // d2, per3, stride=step3), :]
            dst_ref[2, pl.ds(r3 * per3, per3), :] = cls.astype(_BF16)

    for g, (_, d) in enumerate(DILATED_PATTERNS):
        per = tile // d
        for r in range(d):
            for n in range(per // Q_BLOCK):
                base = r * per + n * Q_BLOCK
                q = qs_ref[g, pl.ds(base, Q_BLOCK), :]
                if n > 0:
                    kk = ks_ref[slot, g, pl.ds(base - Q_BLOCK, 2 * Q_BLOCK), :]
                    vv = vs_ref[slot, g, pl.ds(base - Q_BLOCK, 2 * Q_BLOCK), :]
                    prev_ok = in_prev
                else:
                    last = pl.ds((r + 1) * per - Q_BLOCK, Q_BLOCK)
                    kk = jnp.concatenate([ks_ref[1 - slot, g, last, :], ks_ref[slot, g, pl.ds(base, Q_BLOCK), :]], axis=0)
                    vv = jnp.concatenate([vs_ref[1 - slot, g, last, :], vs_ref[slot, g, pl.ds(base, Q_BLOCK), :]], axis=0)
                    prev_ok = in_prev & (t > 0)
                s_prev, s_cur = _lane_chunks(lax.dot_general(q, kk, _NT, preferred_element_type=_F32))
                s_prev = jnp.where(prev_ok, s_prev, NEG_INF)
                s_cur = jnp.where(in_cur, s_cur, NEG_INF)
                m = jnp.broadcast_to(jnp.max(jnp.maximum(s_prev, s_cur), axis=1, keepdims=True), s_cur.shape)
                p = jnp.concatenate([jnp.exp2(s_prev - m), jnp.exp2(s_cur - m)], axis=1).astype(_BF16)
                acc, l = _lane_chunks(jnp.dot(p, jnp.concatenate([vv, ones], axis=1), preferred_element_type=_F32))
                nat = pl.ds(n * Q_BLOCK * d + r, Q_BLOCK) if d == 1 else pl.ds(n * Q_BLOCK * d + r, Q_BLOCK, stride=d)
                out_ref[g, nat, :] = acc / l
                lse_ref[g, nat, :] = m + jnp.log2(l)

    lse_all = jnp.maximum(jnp.maximum(lse_ref[0], lse_ref[1]), lse_ref[2])
    num = jnp.zeros((tile, HEAD_DIM), _F32)
    den = jnp.zeros((tile, LANES), _F32)
    for g in range(len(DILATED_PATTERNS)):
        e = jnp.exp2(lse_ref[g] - lse_all)
        num = num + e * out_ref[g]
        den = den + e
    o_ref[0] = (num / den).astype(o_ref.dtype)


def _dilated_attention(qkv, n_heads):
    b, _, s, _ = qkv.shape
    tile = Q_BLOCK * DILATED_PATTERNS[-1][1]
    assert s % tile == 0
    n_pat = len(DILATED_PATTERNS)
    tiled = lambda part: pl.BlockSpec((None, 1, tile, HEAD_DIM), lambda bi, h, t: (bi, part * n_heads + h, t, 0))
    stats = pltpu.VMEM((n_pat, tile, LANES), _F32)
    return pl.pallas_call(
        functools.partial(_dilated_kernel, tile),
        grid=(b, n_heads, s // tile),
        in_specs=[tiled(0), tiled(1), tiled(2)],
        out_specs=pl.BlockSpec((1, tile, HEAD_DIM), lambda bi, h, t: (bi, t, h)),
        out_shape=jax.ShapeDtypeStruct((b, s, n_heads * HEAD_DIM), _BF16),
        scratch_shapes=[pltpu.VMEM((n_pat, tile, HEAD_DIM), _BF16),
                        pltpu.VMEM((2, n_pat, tile, HEAD_DIM), _BF16),
                        pltpu.VMEM((2, n_pat, tile, HEAD_DIM), _BF16),
                        stats, stats, pltpu.VMEM((tile, HEAD_DIM), _F32)],
        compiler_params=_params(40, 3),
        name="dilated_attention",
    )(qkv, qkv, qkv)


ONES_ROWS = 16


def _fox_kernel(tq, q_ref, k_ref, v_ref, kx_ref, o_ref, vt_ref, m_ref, acc_ref, sa_ref, sb_ref):
    qi = pl.program_id(2)
    seq = v_ref.shape[1]
    tk = tq // 2

    @pl.when(qi == 0)
    def _():
        for c in range(seq // tk):
            blk = v_ref[0, c * tk:(c + 1) * tk, :].astype(_F32)
            vt_ref[:HEAD_DIM, c * tk:(c + 1) * tk] = blk.T.astype(_BF16)
        vt_ref[HEAD_DIM:, :] = jnp.ones((ONES_ROWS, seq), _BF16)

    depth = lax.broadcasted_iota(jnp.int32, (LANES, tq), 0)
    first = BIAS_PIECES * pl.program_id(1)
    ones = ((depth >= first) & (depth < first + BIAS_PIECES)).astype(_BF16)
    q_aug = jnp.concatenate([q_ref[0].astype(_F32).T.astype(_BF16), ones], axis=0)
    m_ref[...] = jnp.full_like(m_ref, NEG_INF)
    acc_ref[...] = jnp.zeros_like(acc_ref)

    def scores(j, s_ref, q0=0):
        k0 = pl.multiple_of(j * tk, tk)
        k_aug = jnp.concatenate([k_ref[0, pl.ds(k0, tk), :], kx_ref[0, pl.ds(k0, tk), :]], axis=1)
        s_ref[:, :tq - q0] = jnp.dot(k_aug, q_aug[:, q0:], preferred_element_type=_F32)

    def update(j, s_ref, q0=0, diagonal=False):
        s = s_ref[:, :tq - q0]
        if diagonal:
            key = lax.broadcasted_iota(jnp.int32, s.shape, 0)
            qry = lax.broadcasted_iota(jnp.int32, s.shape, 1)
            s = jnp.where(key <= qry, s, NEG_INF)
        m_prev = m_ref[:, q0:]
        m_new = jnp.maximum(m_prev, jnp.max(s, axis=0, keepdims=True))
        a = jnp.exp2(m_prev - m_new)
        p = jnp.exp2(s - m_new).astype(_BF16)
        pv = jnp.dot(vt_ref[:, pl.ds(pl.multiple_of(j * tk, tk), tk)], p,
                     preferred_element_type=_F32)
        acc_ref[:, q0:] = a * acc_ref[:, q0:] + pv
        m_ref[:, q0:] = m_new

    def pair(j0):
        scores(j0 + 1, sb_ref)
        update(j0, sa_ref)
        scores(j0 + 2, sa_ref)
        update(j0 + 1, sb_ref)

    def two_pairs(jj, carry):
        pair(4 * jj)
        pair(4 * jj + 2)
        return carry

    scores(0, sa_ref)
    lax.fori_loop(0, qi // 2, two_pairs, 0)

    @pl.when(qi % 2 == 1)
    def _():
        pair(2 * qi - 2)

    scores(2 * qi + 1, sb_ref, q0=tk)
    update(2 * qi, sa_ref, diagonal=True)
    update(2 * qi + 1, sb_ref, q0=tk, diagonal=True)

    out_t = acc_ref[:HEAD_DIM, :] / acc_ref[HEAD_DIM:HEAD_DIM + 1, :]
    o_ref[0] = out_t.T.astype(o_ref.dtype)


def _forgetting_attention(qkv, kx, n_heads):
    b, _, s, _ = qkv.shape
    tq = _pick(s, (1024, 512, 256))
    whole = lambda part: pl.BlockSpec((None, 1, s, HEAD_DIM), lambda bi, h, t: (bi, part * n_heads + h, 0, 0))
    return pl.pallas_call(
        functools.partial(_fox_kernel, tq),
        grid=(b, n_heads, s // tq),
        in_specs=[pl.BlockSpec((None, 1, tq, HEAD_DIM), lambda bi, h, t: (bi, h, t, 0)), whole(1), whole(2),
                  pl.BlockSpec((1, s, LANES), lambda bi, h, t: (bi, 0, 0))],
        out_specs=pl.BlockSpec((1, tq, HEAD_DIM), lambda bi, h, t: (bi, t, h)),
        out_shape=jax.ShapeDtypeStruct((b, s, n_heads * HEAD_DIM), _BF16),
        scratch_shapes=[pltpu.VMEM((HEAD_DIM + ONES_ROWS, s), _BF16),
                        pltpu.VMEM((1, tq), _F32),
                        pltpu.VMEM((HEAD_DIM + ONES_ROWS, tq), _F32),
                        pltpu.VMEM((tq // 2, tq), _F32),
                        pltpu.VMEM((tq // 2, tq), _F32)],
        compiler_params=_params(48, 3),
        name="forgetting_attention",
    )(qkv, qkv, qkv, kx)


def _merge_kernel(alpha, d_model, oa_ref, za_ref, ob_ref, zb_ref, g_ref, x_ref,
                  wa_ref, wb_ref, wo_ref, lng_ref, lnb_ref, o_ref, ob16_ref):
    up_a = jnp.dot(oa_ref[...] * za_ref[...], wa_ref[...], preferred_element_type=_F32)
    up_b = jnp.dot(ob_ref[...] * zb_ref[...], wb_ref[...], preferred_element_type=_F32)
    mix = (g_ref[:, :d_model].astype(_F32) * up_a + g_ref[:, d_model:].astype(_F32) * up_b)
    y = jnp.dot(mix.astype(_BF16), wo_ref[...], preferred_element_type=_F32)
    r = alpha * x_ref[...] + y
    mu = jnp.mean(r, axis=-1, keepdims=True)
    cen = r - mu
    var = jnp.mean(cen * cen, axis=-1, keepdims=True)
    out = cen * lax.rsqrt(var + LN_EPS) * lng_ref[...] + lnb_ref[...]
    o_ref[...] = out
    ob16_ref[...] = out.astype(_BF16)


def _merge(oa, sza, ob, szb, gates, x, w_up_a, w_up_b, w_out, ln_g, ln_b, alpha):
    m, d_model = x.shape
    w = oa.shape[1]
    tm = _pick(m, (256, 128))
    row = lambda n: pl.BlockSpec((tm, n), lambda i: (i, 0))
    const = lambda a, c: pl.BlockSpec((a, c), lambda i: (0, 0), pipeline_mode=pl.Buffered(1))
    return pl.pallas_call(
        functools.partial(_merge_kernel, alpha, d_model),
        grid=(m // tm,),
        in_specs=[row(w), row(w), row(w), row(w), row(2 * d_model), row(d_model),
                  const(w, d_model), const(w, d_model), const(d_model, d_model),
                  const(1, d_model), const(1, d_model)],
        out_specs=[row(d_model), row(d_model)],
        out_shape=[jax.ShapeDtypeStruct((m, d_model), _F32),
                   jax.ShapeDtypeStruct((m, d_model), _BF16)],
        compiler_params=_params(48, 1),
        name="merge_deepnorm",
    )(oa, sza, ob, szb, gates, x, w_up_a, w_up_b, w_out, ln_g, ln_b)


def _rope_tables(seq):
    half = HEAD_DIM // 2
    inv_freq = ROPE_THETA ** (-jnp.arange(half, dtype=_F32) / half)
    ang = jnp.arange(seq, dtype=_F32)[:, None] * inv_freq[None, :]
    cos, sin = jnp.cos(ang), jnp.sin(ang)
    return jnp.concatenate([cos, cos], axis=1), jnp.concatenate([-sin, sin], axis=1)


def _layer(layer, x, xb, tables, w_in, b_forget, b_gate, w_up_a, w_up_b, w_out, ln_g, ln_b, dims):
    batch, seq, d_model, width_a, width_b, alpha = dims
    heads_a, heads_b = width_a // HEAD_DIM, width_b // HEAD_DIM
    m = batch * seq
    q_scale = LOG2E * HEAD_DIM ** -0.5
    o_za, o_qb = 3 * width_a, 4 * width_a
    o_zb, o_f = o_qb + 3 * width_b, o_qb + 4 * width_b
    o_g = o_f + heads_b
    ones = lambda n: jnp.ones((1, n), _F32)
    project = functools.partial(_project, xb=xb, w=w_in, layer=layer, seq=seq)

    scale_a = jnp.concatenate([jnp.full((1, width_a), q_scale, _F32), ones(2 * width_a)], axis=1)
    qkv_a = project("rope", col0=0, n=3 * width_a, extras=(*tables, scale_a), out_dtype=_F32,
                    rope_cols=2 * width_a)
    sz_a = project("silu", col0=o_za, n=width_a, extras=(), out_dtype=_BF16)
    out_a = _dilated_attention(qkv_a, heads_a)

    scale_b = jnp.concatenate([jnp.full((1, width_b), q_scale, _F32), ones(2 * width_b)], axis=1)
    qkv_b = project("linear", col0=o_qb, n=3 * width_b, extras=(scale_b,), out_dtype=_BF16)
    sz_b = project("silu", col0=o_zb, n=width_b, extras=(), out_dtype=_BF16)
    w_f = jnp.pad(w_in[layer, :, o_f:o_g], ((0, 0), (0, LANES - heads_b)))
    b_f = jnp.pad(b_forget, (0, LANES - heads_b))[None, :]
    kx = _forget_bias(xb, w_f, b_f, batch, seq, heads_b)
    out_b = _forgetting_attention(qkv_b, kx.reshape(batch, seq, LANES), heads_b)

    w_g = w_in[layer, :, o_g:o_g + 2 * d_model][None]
    gates = _project("gate", xb, w_g, 0, 0, 2 * d_model, (b_gate[None, :],), _BF16, seq)
    return _merge(out_a.reshape(m, width_a), sz_a, out_b.reshape(m, width_b), sz_b, gates, x,
                  w_up_a.astype(_BF16), w_up_b.astype(_BF16), w_out.astype(_BF16),
                  ln_g[None, :], ln_b[None, :], alpha)


def kernel(x, w_in, b_forget, b_gate, w_up_a, w_up_b, w_out, ln_g, ln_b):
    batch, seq, d_model = x.shape
    depth = w_in.shape[0]
    width_a, width_b = w_up_a.shape[1], w_up_b.shape[1]
    alpha = float((2 * depth) ** 0.25)
    dims = (batch, seq, d_model, width_a, width_b, alpha)
    tables = _rope_tables(seq)
    x2 = x.reshape(batch * seq, d_model)
    xb = x2.astype(_BF16)
    for l in range(depth):
        x2, xb = _layer(l, x2, xb, tables, w_in, b_forget[l], b_gate[l],
                        w_up_a[l], w_up_b[l], w_out[l], ln_g[l], ln_b[l], dims)
    return x2.reshape(batch, seq, d_model)
```

```python
import functools
import math

import jax
import jax.numpy as jnp
from jax import lax
from jax.experimental import pallas as pl
from jax.experimental.pallas import tpu as pltpu

HEAD_DIM = 128
Q_BLOCK = 128
DILATED_PATTERNS = ((128, 1), (512, 4), (2048, 16))
ROPE_THETA = 10000.0
LN_EPS = 1e-5
LANES = 128
NEG_INF = float("-inf")
LOG2E = math.log2(math.e)
BIAS_PIECES = 3

_F32 = jnp.float32
_BF16 = jnp.bfloat16
_NT = (((1,), (1,)), ((), ()))


def _pick(n, candidates):
    for c in candidates:
        if n % c == 0:
            return c
    raise ValueError(f"no tile in {candidates} divides {n}")


def _params(vmem_mib, n_axes):
    return pltpu.CompilerParams(dimension_semantics=("arbitrary",) * n_axes,
                                vmem_limit_bytes=vmem_mib * 1024 * 1024)


def _lane_chunks(a):
    return [a[:, c * LANES:(c + 1) * LANES] for c in range(a.shape[1] // LANES)]


def _split_bf16(a):
    hi = a.astype(_BF16).astype(_F32)
    mid = (a - hi).astype(_BF16).astype(_F32)
    return hi, mid, a - hi - mid


PROJ_SUB_ROWS = 256


def _proj_kernel(kind, rope_tiles, x_ref, w_ref, *rest):
    o_ref = rest[-1]
    w = w_ref[0].astype(_BF16)
    if kind == "rope":
        cos_ref, sin_ref, scale_ref = rest[:3]
        on = jnp.where(pl.program_id(1) < rope_tiles, 1.0, 0.0)
    def put_heads(rows, c, val):
        o_ref[c, rows, :] = val.astype(o_ref.dtype)

    for rb in range(x_ref.shape[0] // PROJ_SUB_ROWS):
        rows = slice(rb * PROJ_SUB_ROWS, (rb + 1) * PROJ_SUB_ROWS)
        acc = jnp.dot(x_ref[rows, :], w, preferred_element_type=_F32)
        if kind == "rope":
            cos = 1.0 + on * (cos_ref[rows, :] - 1.0)
            sin = on * sin_ref[rows, :]
            for c, t in enumerate(_lane_chunks(acc)):
                r = t * cos + pltpu.roll(t, HEAD_DIM // 2, axis=1) * sin
                put_heads(rows, c, r * scale_ref[:, c * LANES:(c + 1) * LANES])
        elif kind == "linear":
            for c, t in enumerate(_lane_chunks(acc * rest[0][...])):
                put_heads(rows, c, t)
        elif kind == "silu":
            o_ref[rows, :] = (acc * jax.nn.sigmoid(acc)).astype(o_ref.dtype)
        elif kind == "gate":
            o_ref[rows, :] = jax.nn.sigmoid(acc + rest[0][...]).astype(o_ref.dtype)
        else:
            raise ValueError(kind)


def _project(kind, xb, w, layer, col0, n, extras, out_dtype, seq, rope_cols=0):
    m, k = xb.shape
    tm = _pick(seq, (2048, 1024, 512, 256))
    tn = _pick(math.gcd(math.gcd(n, rope_cols), col0), (1024, 768, 512, 256, 128))
    j0 = col0 // tn
    nrow = seq // tm
    if kind in ("rope", "linear"):
        out_spec = pl.BlockSpec((None, tn // LANES, tm, LANES), lambda i, j: (i // nrow, j, i % nrow, 0))
        out_shape = jax.ShapeDtypeStruct((m // seq, n // LANES, seq, LANES), out_dtype)
    else:
        out_spec = pl.BlockSpec((tm, tn), lambda i, j: (i, j))
        out_shape = jax.ShapeDtypeStruct((m, n), out_dtype)
    in_specs = [pl.BlockSpec((tm, k), lambda i, j: (i, 0)),
                pl.BlockSpec((1, k, tn), lambda i, j: (layer, 0, j0 + j))]
    if kind == "rope":
        table = pl.BlockSpec((tm, HEAD_DIM), lambda i, j: (i % nrow, 0))
        in_specs += [table, table, pl.BlockSpec((1, tn), lambda i, j: (0, j))]
    else:
        in_specs += [pl.BlockSpec((1, tn), lambda i, j: (0, j)) for _ in extras]
    return pl.pallas_call(
        functools.partial(_proj_kernel, kind, rope_cols // tn),
        grid=(m // tm, n // tn),
        in_specs=in_specs,
        out_specs=out_spec,
        out_shape=out_shape,
        compiler_params=_params(56, 2),
        name=f"proj_{kind}",
    )(xb, w, *extras)


def _forget_kernel(tiles_per_seq, n_heads, x_ref, w_ref, b_ref, tri_ref, o_ref, carry_ref):
    @pl.when(pl.program_id(0) % tiles_per_seq == 0)
    def _():
        carry_ref[...] = jnp.zeros_like(carry_ref)

    z = jnp.dot(x_ref[...], w_ref[...].astype(_BF16), preferred_element_type=_F32) + b_ref[...]
    log_f = jnp.minimum(z, 0.0) - jnp.log1p(jnp.exp(-jnp.abs(z)))
    tri = tri_ref[...]
    cum = carry_ref[...]
    for piece in _split_bf16(log_f):
        cum = cum + jnp.dot(tri, piece.astype(_BF16), preferred_element_type=_F32)
    carry_ref[...] = cum[-1:, :]
    head = lax.broadcasted_iota(jnp.int32, (LANES, LANES), 0)
    lane = lax.broadcasted_iota(jnp.int32, (LANES, LANES), 1)
    out = jnp.zeros(cum.shape, _F32)
    for k, piece in enumerate(_split_bf16(cum * (-LOG2E))):
        place = ((lane == BIAS_PIECES * head + k) & (head < n_heads)).astype(_BF16)
        out = out + jnp.dot(piece.astype(_BF16), place, preferred_element_type=_F32)
    o_ref[...] = out.astype(_BF16)


def _forget_bias(xb, w_f, b_f, batch, seq, n_heads):
    assert BIAS_PIECES * n_heads <= LANES
    m, k = xb.shape
    tm = _pick(seq, (512, 256, 128))
    tiles = seq // tm
    tri = (lax.broadcasted_iota(jnp.int32, (tm, tm), 0)
           >= lax.broadcasted_iota(jnp.int32, (tm, tm), 1)).astype(_BF16)
    return pl.pallas_call(
        functools.partial(_forget_kernel, tiles, n_heads),
        grid=(m // tm,),
        in_specs=[pl.BlockSpec((tm, k), lambda i: (i, 0)),
                  pl.BlockSpec((k, LANES), lambda i: (0, 0)),
                  pl.BlockSpec((1, LANES), lambda i: (0, 0)),
                  pl.BlockSpec((tm, tm), lambda i: (0, 0))],
        out_specs=pl.BlockSpec((tm, LANES), lambda i: (i, 0)),
        out_shape=jax.ShapeDtypeStruct((m, LANES), _BF16),
        scratch_shapes=[pltpu.VMEM((1, LANES), _F32)],
        compiler_params=_params(32, 1),
        name="forget_bias",
    )(xb, w_f, b_f, tri)


def _dilated_kernel(tile, q_ref, k_ref, v_ref, o_ref, qs_ref, ks_ref, vs_ref, out_ref, lse_ref, mid_ref):
    t = pl.program_id(2)
    slot = t % 2
    row = lax.broadcasted_iota(jnp.int32, (Q_BLOCK, LANES), 0)
    lane = lax.broadcasted_iota(jnp.int32, (Q_BLOCK, LANES), 1)
    in_prev = lane >= row
    in_cur = lane <= row
    ones = jnp.ones((2 * Q_BLOCK, LANES), _BF16)

    @pl.when(t == 0)
    def _():
        vs_ref[1] = jnp.zeros(vs_ref.shape[1:], _BF16)

    (_, d1), (_, d2), (_, d3) = DILATED_PATTERNS
    assert d1 == 1 and d3 % d2 == 0
    per2, per3, step3 = tile // d2, tile // d3, d3 // d2
    for src_ref, dst_ref in ((q_ref, qs_ref), (k_ref, ks_ref.at[slot]), (v_ref, vs_ref.at[slot])):
        dst_ref[0] = src_ref[0].astype(_BF16)
        for r2 in range(d2):
            cls = src_ref[0, pl.ds(r2, per2, stride=d2), :]
            mid_ref[pl.ds(r2 * per2, per2), :] = cls
            dst_ref[1, pl.ds(r2 * per2, per2), :] = cls.astype(_BF16)
        for r3 in range(d3):
            cls = mid_ref[pl.ds((r3 % d2) * per2 + r3 // d2, per3, stride=step3), :]
            dst_ref[2, pl.ds(r3 * per3, per3), :] = cls.astype(_BF16)

    for g, (_, d) in enumerate(DILATED_PATTERNS):
        per = tile // d
        for r in range(d):
            for n in range(per // Q_BLOCK):
                base = r * per + n * Q_BLOCK
                q = qs_ref[g, pl.ds(base, Q_BLOCK), :]
                if n > 0:
                    kk = ks_ref[slot, g, pl.ds(base - Q_BLOCK, 2 * Q_BLOCK), :]
                    vv = vs_ref[slot, g, pl.ds(base - Q_BLOCK, 2 * Q_BLOCK), :]
                    prev_ok = in_prev
                else:
                    last = pl.ds((r + 1) * per - Q_BLOCK, Q_BLOCK)
                    kk = jnp.concatenate([ks_ref[1 - slot, g, last, :], ks_ref[slot, g, pl.ds(base, Q_BLOCK), :]], axis=0)
                    vv = jnp.concatenate([vs_ref[1 - slot, g, last, :], vs_ref[slot, g, pl.ds(base, Q_BLOCK), :]], axis=0)
                    prev_ok = in_prev & (t > 0)
                s_prev, s_cur = _lane_chunks(lax.dot_general(q, kk, _NT, preferred_element_type=_F32))
                s_prev = jnp.where(prev_ok, s_prev, NEG_INF)
                s_cur = jnp.where(in_cur, s_cur, NEG_INF)
                m = jnp.broadcast_to(jnp.max(jnp.maximum(s_prev, s_cur), axis=1, keepdims=True), s_cur.shape)
                p = jnp.concatenate([jnp.exp2(s_prev - m), jnp.exp2(s_cur - m)], axis=1).astype(_BF16)
                acc, l = _lane_chunks(jnp.dot(p, jnp.concatenate([vv, ones], axis=1), preferred_element_type=_F32))
                nat = pl.ds(n * Q_BLOCK * d + r, Q_BLOCK) if d == 1 else pl.ds(n * Q_BLOCK * d + r, Q_BLOCK, stride=d)
                out_ref[g, nat, :] = acc / l
                lse_ref[g, nat, :] = m + jnp.log2(l)

    lse_all = jnp.maximum(jnp.maximum(lse_ref[0], lse_ref[1]), lse_ref[2])
    num = jnp.zeros((tile, HEAD_DIM), _F32)
    den = jnp.zeros((tile, LANES), _F32)
    for g in range(len(DILATED_PATTERNS)):
        e = jnp.exp2(lse_ref[g] - lse_all)
        num = num + e * out_ref[g]
        den = den + e
    o_ref[0] = (num / den).astype(o_ref.dtype)


def _dilated_attention(qkv, n_heads):
    b, _, s, _ = qkv.shape
    tile = Q_BLOCK * DILATED_PATTERNS[-1][1]
    assert s % tile == 0
    n_pat = len(DILATED_PATTERNS)
    tiled = lambda part: pl.BlockSpec((None, 1, tile, HEAD_DIM), lambda bi, h, t: (bi, part * n_heads + h, t, 0))
    stats = pltpu.VMEM((n_pat, tile, LANES), _F32)
    return pl.pallas_call(
        functools.partial(_dilated_kernel, tile),
        grid=(b, n_heads, s // tile),
        in_specs=[tiled(0), tiled(1), tiled(2)],
        out_specs=pl.BlockSpec((1, tile, HEAD_DIM), lambda bi, h, t: (bi, t, h)),
        out_shape=jax.ShapeDtypeStruct((b, s, n_heads * HEAD_DIM), _BF16),
        scratch_shapes=[pltpu.VMEM((n_pat, tile, HEAD_DIM), _BF16),
                        pltpu.VMEM((2, n_pat, tile, HEAD_DIM), _BF16),
                        pltpu.VMEM((2, n_pat, tile, HEAD_DIM), _BF16),
                        stats, stats, pltpu.VMEM((tile, HEAD_DIM), _F32)],
        compiler_params=_params(40, 3),
        name="dilated_attention",
    )(qkv, qkv, qkv)


ONES_ROWS = 16
FOX_KEY_BLOCK = 512


def _fox_kernel(tq, q_ref, k_ref, v_ref, kx_ref, o_ref, vt_ref, m_ref, acc_ref, sa_ref, sb_ref):
    qi = pl.program_id(2)
    seq = v_ref.shape[1]
    tk = FOX_KEY_BLOCK
    n_sub = tq // tk

    @pl.when(qi == 0)
    def _():
        for c in range(seq // tk):
            blk = v_ref[0, c * tk:(c + 1) * tk, :].astype(_F32)
            vt_ref[:HEAD_DIM, c * tk:(c + 1) * tk] = blk.T.astype(_BF16)
        vt_ref[HEAD_DIM:, :] = jnp.ones((ONES_ROWS, seq), _BF16)

    depth = lax.broadcasted_iota(jnp.int32, (LANES, tq), 0)
    first = BIAS_PIECES * pl.program_id(1)
    ones = ((depth >= first) & (depth < first + BIAS_PIECES)).astype(_BF16)
    q_aug = jnp.concatenate([q_ref[0].astype(_F32).T.astype(_BF16), ones], axis=0)
    m_ref[...] = jnp.full_like(m_ref, NEG_INF)
    acc_ref[...] = jnp.zeros_like(acc_ref)

    def scores(j, s_ref, q0=0):
        k0 = pl.multiple_of(j * tk, tk)
        k_aug = jnp.concatenate([k_ref[0, pl.ds(k0, tk), :], kx_ref[0, pl.ds(k0, tk), :]], axis=1)
        s_ref[:, :tq - q0] = jnp.dot(k_aug, q_aug[:, q0:], preferred_element_type=_F32)

    def update(j, s_ref, q0=0, diagonal=False):
        vt = vt_ref[:, pl.ds(pl.multiple_of(j * tk, tk), tk)]
        for c in range((tq - q0) // tk):
            s = s_ref[:, c * tk:(c + 1) * tk]
            if diagonal and c == 0:
                key = lax.broadcasted_iota(jnp.int32, s.shape, 0)
                qry = lax.broadcasted_iota(jnp.int32, s.shape, 1)
                s = jnp.where(key <= qry, s, NEG_INF)
            cols = slice(q0 + c * tk, q0 + (c + 1) * tk)
            m_prev = m_ref[:, cols]
            m_new = jnp.maximum(m_prev, jnp.max(s, axis=0, keepdims=True))
            a = jnp.exp2(m_prev - m_new)
            p = jnp.exp2(s - m_new).astype(_BF16)
            acc_ref[:, cols] = a * acc_ref[:, cols] + jnp.dot(vt, p, preferred_element_type=_F32)
            m_ref[:, cols] = m_new

    def pair(j0):
        scores(j0 + 1, sb_ref)
        update(j0, sa_ref)
        scores(j0 + 2, sa_ref)
        update(j0 + 1, sb_ref)

    def below_diagonal(jj, carry):
        for u in range(0, n_sub, 2):
            pair(n_sub * jj + u)
        return carry

    scores(0, sa_ref)
    lax.fori_loop(0, qi, below_diagonal, 0)
    bufs = (sa_ref, sb_ref)
    for d in range(n_sub):
        if d + 1 < n_sub:
            scores(n_sub * qi + d + 1, bufs[(d + 1) % 2], q0=(d + 1) * tk)
        update(n_sub * qi + d, bufs[d % 2], q0=d * tk, diagonal=True)

    out_t = acc_ref[:HEAD_DIM, :] / acc_ref[HEAD_DIM:HEAD_DIM + 1, :]
    o_ref[0] = out_t.T.astype(o_ref.dtype)


def _forgetting_attention(qkv, kx, n_heads):
    b, _, s, _ = qkv.shape
    tq = _pick(s, (4 * FOX_KEY_BLOCK, 2 * FOX_KEY_BLOCK))
    whole = lambda part: pl.BlockSpec((None, 1, s, HEAD_DIM), lambda bi, h, t: (bi, part * n_heads + h, 0, 0))
    return pl.pallas_call(
        functools.partial(_fox_kernel, tq),
        grid=(b, n_heads, s // tq),
        in_specs=[pl.BlockSpec((None, 1, tq, HEAD_DIM), lambda bi, h, t: (bi, h, t, 0)), whole(1), whole(2),
                  pl.BlockSpec((1, s, LANES), lambda bi, h, t: (bi, 0, 0))],
        out_specs=pl.BlockSpec((1, tq, HEAD_DIM), lambda bi, h, t: (bi, t, h)),
        out_shape=jax.ShapeDtypeStruct((b, s, n_heads * HEAD_DIM), _BF16),
        scratch_shapes=[pltpu.VMEM((HEAD_DIM + ONES_ROWS, s), _BF16),
                        pltpu.VMEM((1, tq), _F32),
                        pltpu.VMEM((HEAD_DIM + ONES_ROWS, tq), _F32),
                        pltpu.VMEM((FOX_KEY_BLOCK, tq), _F32),
                        pltpu.VMEM((FOX_KEY_BLOCK, tq), _F32)],
        compiler_params=_params(48, 3),
        name="forgetting_attention",
    )(qkv, qkv, qkv, kx)


def _merge_kernel(alpha, d_model, oa_ref, za_ref, ob_ref, zb_ref, g_ref, x_ref,
                  wa_ref, wb_ref, wo_ref, lng_ref, lnb_ref, o_ref, ob16_ref):
    up_a = jnp.dot(oa_ref[...] * za_ref[...], wa_ref[...], preferred_element_type=_F32)
    up_b = jnp.dot(ob_ref[...] * zb_ref[...], wb_ref[...], preferred_element_type=_F32)
    mix = (g_ref[:, :d_model].astype(_F32) * up_a + g_ref[:, d_model:].astype(_F32) * up_b)
    y = jnp.dot(mix.astype(_BF16), wo_ref[...], preferred_element_type=_F32)
    r = alpha * x_ref[...] + y
    mu = jnp.mean(r, axis=-1, keepdims=True)
    cen = r - mu
    var = jnp.mean(cen * cen, axis=-1, keepdims=True)
    out = cen * lax.rsqrt(var + LN_EPS) * lng_ref[...] + lnb_ref[...]
    o_ref[...] = out
    ob16_ref[...] = out.astype(_BF16)


def _merge(oa, sza, ob, szb, gates, x, w_up_a, w_up_b, w_out, ln_g, ln_b, alpha):
    m, d_model = x.shape
    w = oa.shape[1]
    tm = _pick(m, (256, 128))
    row = lambda n: pl.BlockSpec((tm, n), lambda i: (i, 0))
    const = lambda a, c: pl.BlockSpec((a, c), lambda i: (0, 0), pipeline_mode=pl.Buffered(1))
    return pl.pallas_call(
        functools.partial(_merge_kernel, alpha, d_model),
        grid=(m // tm,),
        in_specs=[row(w), row(w), row(w), row(w), row(2 * d_model), row(d_model),
                  const(w, d_model), const(w, d_model), const(d_model, d_model),
                  const(1, d_model), const(1, d_model)],
        out_specs=[row(d_model), row(d_model)],
        out_shape=[jax.ShapeDtypeStruct((m, d_model), _F32),
                   jax.ShapeDtypeStruct((m, d_model), _BF16)],
        compiler_params=_params(48, 1),
        name="merge_deepnorm",
    )(oa, sza, ob, szb, gates, x, w_up_a, w_up_b, w_out, ln_g, ln_b)


def _rope_tables(seq):
    half = HEAD_DIM // 2
    inv_freq = ROPE_THETA ** (-jnp.arange(half, dtype=_F32) / half)
    ang = jnp.arange(seq, dtype=_F32)[:, None] * inv_freq[None, :]
    cos, sin = jnp.cos(ang), jnp.sin(ang)
    return jnp.concatenate([cos, cos], axis=1), jnp.concatenate([-sin, sin], axis=1)


def _layer(layer, x, xb, tables, w_in, b_forget, b_gate, w_up_a, w_up_b, w_out, ln_g, ln_b, dims):
    batch, seq, d_model, width_a, width_b, alpha = dims
    heads_a, heads_b = width_a // HEAD_DIM, width_b // HEAD_DIM
    m = batch * seq
    q_scale = LOG2E * HEAD_DIM ** -0.5
    o_za, o_qb = 3 * width_a, 4 * width_a
    o_zb, o_f = o_qb + 3 * width_b, o_qb + 4 * width_b
    o_g = o_f + heads_b
    ones = lambda n: jnp.ones((1, n), _F32)
    project = functools.partial(_project, xb=xb, w=w_in, layer=layer, seq=seq)

    scale_a = jnp.concatenate([jnp.full((1, width_a), q_scale, _F32), ones(2 * width_a)], axis=1)
    qkv_a = project("rope", col0=0, n=3 * width_a, extras=(*tables, scale_a), out_dtype=_F32,
                    rope_cols=2 * width_a)
    sz_a = project("silu", col0=o_za, n=width_a, extras=(), out_dtype=_BF16)
    out_a = _dilated_attention(qkv_a, heads_a)

    scale_b = jnp.concatenate([jnp.full((1, width_b), q_scale, _F32), ones(2 * width_b)], axis=1)
    qkv_b = project("linear", col0=o_qb, n=3 * width_b, extras=(scale_b,), out_dtype=_BF16)
    sz_b = project("silu", col0=o_zb, n=width_b, extras=(), out_dtype=_BF16)
    w_f = jnp.pad(w_in[layer, :, o_f:o_g], ((0, 0), (0, LANES - heads_b)))
    b_f = jnp.pad(b_forget, (0, LANES - heads_b))[None, :]
    kx = _forget_bias(xb, w_f, b_f, batch, seq, heads_b)
    out_b = _forgetting_attention(qkv_b, kx.reshape(batch, seq, LANES), heads_b)

    w_g = w_in[layer, :, o_g:o_g + 2 * d_model][None]
    gates = _project("gate", xb, w_g, 0, 0, 2 * d_model, (b_gate[None, :],), _BF16, seq)
    return _merge(out_a.reshape(m, width_a), sz_a, out_b.reshape(m, width_b), sz_b, gates, x,
                  w_up_a.astype(_BF16), w_up_b.astype(_BF16), w_out.astype(_BF16),
                  ln_g[None, :], ln_b[None, :], alpha)


def kernel(x, w_in, b_forget, b_gate, w_up_a, w_up_b, w_out, ln_g, ln_b):
    batch, seq, d_model = x.shape
    depth = w_in.shape[0]
    width_a, width_b = w_up_a.shape[1], w_up_b.shape[1]
    alpha = float((2 * depth) ** 0.25)
    dims = (batch, seq, d_model, width_a, width_b, alpha)
    tables = _rope_tables(seq)
    x2 = x.reshape(batch * seq, d_model)
    xb = x2.astype(_BF16)
    for l in range(depth):
        x2, xb = _layer(l, x2, xb, tables, w_in, b_forget[l], b_gate[l],
                        w_up_a[l], w_up_b[l], w_out[l], ln_g[l], ln_b[l], dims)
    return x2.reshape(batch, seq, d_model)
```

```python
import functools
import math

import jax
import jax.numpy as jnp
from jax import lax
from jax.experimental import pallas as pl
from jax.experimental.pallas import tpu as pltpu

HEAD_DIM = 128
Q_BLOCK = 128
DILATED_PATTERNS = ((128, 1), (512, 4), (2048, 16))
ROPE_THETA = 10000.0
LN_EPS = 1e-5
LANES = 128
NEG_INF = float("-inf")
LOG2E = math.log2(math.e)
BIAS_PIECES = 3

_F32 = jnp.float32
_BF16 = jnp.bfloat16
_NT = (((1,), (1,)), ((), ()))


def _pick(n, candidates):
    for c in candidates:
        if n % c == 0:
            return c
    raise ValueError(f"no tile in {candidates} divides {n}")


def _params(vmem_mib, n_axes):
    return pltpu.CompilerParams(dimension_semantics=("arbitrary",) * n_axes,
                                vmem_limit_bytes=vmem_mib * 1024 * 1024)


def _lane_chunks(a):
    return [a[:, c * LANES:(c + 1) * LANES] for c in range(a.shape[1] // LANES)]


def _split_bf16(a):
    hi = a.astype(_BF16).astype(_F32)
    mid = (a - hi).astype(_BF16).astype(_F32)
    return hi, mid, a - hi - mid


PROJ_SUB_ROWS = 256


def _proj_kernel(kind, rope_tiles, x_ref, w_ref, *rest):
    o_ref = rest[-1]
    w = w_ref[0].astype(_BF16)
    if kind == "rope":
        cos_ref, sin_ref, scale_ref = rest[:3]
        on = jnp.where(pl.program_id(1) < rope_tiles, 1.0, 0.0)
    def put_heads(rows, c, val):
        o_ref[c, rows, :] = val.astype(o_ref.dtype)

    for rb in range(x_ref.shape[0] // PROJ_SUB_ROWS):
        rows = slice(rb * PROJ_SUB_ROWS, (rb + 1) * PROJ_SUB_ROWS)
        acc = jnp.dot(x_ref[rows, :], w, preferred_element_type=_F32)
        if kind == "rope":
            cos = 1.0 + on * (cos_ref[rows, :] - 1.0)
            sin = on * sin_ref[rows, :]
            for c, t in enumerate(_lane_chunks(acc)):
                r = t * cos + pltpu.roll(t, HEAD_DIM // 2, axis=1) * sin
                put_heads(rows, c, r * scale_ref[:, c * LANES:(c + 1) * LANES])
        elif kind == "linear":
            for c, t in enumerate(_lane_chunks(acc * rest[0][...])):
                put_heads(rows, c, t)
        elif kind == "silu":
            o_ref[rows, :] = (acc * jax.nn.sigmoid(acc)).astype(o_ref.dtype)
        elif kind == "gate":
            o_ref[rows, :] = jax.nn.sigmoid(acc + rest[0][...]).astype(o_ref.dtype)
        else:
            raise ValueError(kind)


def _project(kind, xb, w, layer, col0, n, extras, out_dtype, seq, rope_cols=0):
    m, k = xb.shape
    tm = _pick(seq, (2048, 1024, 512, 256))
    tn = _pick(math.gcd(math.gcd(n, rope_cols), col0), (1024, 768, 512, 256, 128))
    j0 = col0 // tn
    nrow = seq // tm
    if kind in ("rope", "linear"):
        out_spec = pl.BlockSpec((None, tn // LANES, tm, LANES), lambda i, j: (i // nrow, j, i % nrow, 0))
        out_shape = jax.ShapeDtypeStruct((m // seq, n // LANES, seq, LANES), out_dtype)
    else:
        out_spec = pl.BlockSpec((tm, tn), lambda i, j: (i, j))
        out_shape = jax.ShapeDtypeStruct((m, n), out_dtype)
    in_specs = [pl.BlockSpec((tm, k), lambda i, j: (i, 0)),
                pl.BlockSpec((1, k, tn), lambda i, j: (layer, 0, j0 + j))]
    if kind == "rope":
        table = pl.BlockSpec((tm, HEAD_DIM), lambda i, j: (i % nrow, 0))
        in_specs += [table, table, pl.BlockSpec((1, tn), lambda i, j: (0, j))]
    else:
        in_specs += [pl.BlockSpec((1, tn), lambda i, j: (0, j)) for _ in extras]
    return pl.pallas_call(
        functools.partial(_proj_kernel, kind, rope_cols // tn),
        grid=(m // tm, n // tn),
        in_specs=in_specs,
        out_specs=out_spec,
        out_shape=out_shape,
        compiler_params=_params(56, 2),
        name=f"proj_{kind}",
    )(xb, w, *extras)


def _forget_kernel(tiles_per_seq, n_heads, x_ref, w_ref, b_ref, tri_ref, o_ref, carry_ref):
    @pl.when(pl.program_id(0) % tiles_per_seq == 0)
    def _():
        carry_ref[...] = jnp.zeros_like(carry_ref)

    z = jnp.dot(x_ref[...], w_ref[...].astype(_BF16), preferred_element_type=_F32) + b_ref[...]
    log_f = jnp.minimum(z, 0.0) - jnp.log1p(jnp.exp(-jnp.abs(z)))
    tri = tri_ref[...]
    cum = carry_ref[...]
    for piece in _split_bf16(log_f):
        cum = cum + jnp.dot(tri, piece.astype(_BF16), preferred_element_type=_F32)
    carry_ref[...] = cum[-1:, :]
    head = lax.broadcasted_iota(jnp.int32, (LANES, LANES), 0)
    lane = lax.broadcasted_iota(jnp.int32, (LANES, LANES), 1)
    out = jnp.zeros(cum.shape, _F32)
    for k, piece in enumerate(_split_bf16(cum * (-LOG2E))):
        place = ((lane == BIAS_PIECES * head + k) & (head < n_heads)).astype(_BF16)
        out = out + jnp.dot(piece.astype(_BF16), place, preferred_element_type=_F32)
    o_ref[...] = out.astype(_BF16)


def _forget_bias(xb, w_f, b_f, batch, seq, n_heads):
    assert BIAS_PIECES * n_heads <= LANES
    m, k = xb.shape
    tm = _pick(seq, (1024, 512, 256))
    tiles = seq // tm
    tri = (lax.broadcasted_iota(jnp.int32, (tm, tm), 0)
           >= lax.broadcasted_iota(jnp.int32, (tm, tm), 1)).astype(_BF16)
    return pl.pallas_call(
        functools.partial(_forget_kernel, tiles, n_heads),
        grid=(m // tm,),
        in_specs=[pl.BlockSpec((tm, k), lambda i: (i, 0)),
                  pl.BlockSpec((k, LANES), lambda i: (0, 0)),
                  pl.BlockSpec((1, LANES), lambda i: (0, 0)),
                  pl.BlockSpec((tm, tm), lambda i: (0, 0))],
        out_specs=pl.BlockSpec((tm, LANES), lambda i: (i, 0)),
        out_shape=jax.ShapeDtypeStruct((m, LANES), _BF16),
        scratch_shapes=[pltpu.VMEM((1, LANES), _F32)],
        compiler_params=_params(32, 1),
        name="forget_bias",
    )(xb, w_f, b_f, tri)


def _dilated_kernel(tile, q_ref, k_ref, v_ref, o_ref, qs_ref, ks_ref, vs_ref, out_ref, lse_ref, mid_ref):
    t = pl.program_id(2)
    slot = t % 2
    row = lax.broadcasted_iota(jnp.int32, (Q_BLOCK, LANES), 0)
    lane = lax.broadcasted_iota(jnp.int32, (Q_BLOCK, LANES), 1)
    in_prev = lane >= row
    in_cur = lane <= row
    ones = jnp.ones((2 * Q_BLOCK, LANES), _BF16)

    @pl.when(t == 0)
    def _():
        vs_ref[1] = jnp.zeros(vs_ref.shape[1:], _BF16)

    (_, d1), (_, d2), (_, d3) = DILATED_PATTERNS
    assert d1 == 1 and d3 % d2 == 0
    per2, per3, step3 = tile // d2, tile // d3, d3 // d2
    for src_ref, dst_ref in ((q_ref, qs_ref), (k_ref, ks_ref.at[slot]), (v_ref, vs_ref.at[slot])):
        dst_ref[0] = src_ref[0].astype(_BF16)
        for r2 in range(d2):
            cls = src_ref[0, pl.ds(r2, per2, stride=d2), :]
            mid_ref[pl.ds(r2 * per2, per2), :] = cls
            dst_ref[1, pl.ds(r2 * per2, per2), :] = cls.astype(_BF16)
        for r3 in range(d3):
            cls = mid_ref[pl.ds((r3 % d2) * per2 + r3 // d2, per3, stride=step3), :]
            dst_ref[2, pl.ds(r3 * per3, per3), :] = cls.astype(_BF16)

    for g, (_, d) in enumerate(DILATED_PATTERNS):
        per = tile // d
        for r in range(d):
            for n in range(per // Q_BLOCK):
                base = r * per + n * Q_BLOCK
                q = qs_ref[g, pl.ds(base, Q_BLOCK), :]
                if n > 0:
                    kk = ks_ref[slot, g, pl.ds(base - Q_BLOCK, 2 * Q_BLOCK), :]
                    vv = vs_ref[slot, g, pl.ds(base - Q_BLOCK, 2 * Q_BLOCK), :]
                    prev_ok = in_prev
                else:
                    last = pl.ds((r + 1) * per - Q_BLOCK, Q_BLOCK)
                    kk = jnp.concatenate([ks_ref[1 - slot, g, last, :], ks_ref[slot, g, pl.ds(base, Q_BLOCK), :]], axis=0)
                    vv = jnp.concatenate([vs_ref[1 - slot, g, last, :], vs_ref[slot, g, pl.ds(base, Q_BLOCK), :]], axis=0)
                    prev_ok = in_prev & (t > 0)
                s_prev, s_cur = _lane_chunks(lax.dot_general(q, kk, _NT, preferred_element_type=_F32))
                s_prev = jnp.where(prev_ok, s_prev, NEG_INF)
                s_cur = jnp.where(in_cur, s_cur, NEG_INF)
                m = jnp.broadcast_to(jnp.max(jnp.maximum(s_prev, s_cur), axis=1, keepdims=True), s_cur.shape)
                p = jnp.concatenate([jnp.exp2(s_prev - m), jnp.exp2(s_cur - m)], axis=1).astype(_BF16)
                acc, l = _lane_chunks(jnp.dot(p, jnp.concatenate([vv, ones], axis=1), preferred_element_type=_F32))
                nat = pl.ds(n * Q_BLOCK * d + r, Q_BLOCK) if d == 1 else pl.ds(n * Q_BLOCK * d + r, Q_BLOCK, stride=d)
                out_ref[g, nat, :] = acc / l
                lse_ref[g, nat, :] = m + jnp.log2(l)

    lse_all = jnp.maximum(jnp.maximum(lse_ref[0], lse_ref[1]), lse_ref[2])
    num = jnp.zeros((tile, HEAD_DIM), _F32)
    den = jnp.zeros((tile, LANES), _F32)
    for g in range(len(DILATED_PATTERNS)):
        e = jnp.exp2(lse_ref[g] - lse_all)
        num = num + e * out_ref[g]
        den = den + e
    o_ref[0] = (num / den).astype(o_ref.dtype)


def _dilated_attention(qkv, n_heads):
    b, _, s, _ = qkv.shape
    tile = 2 * Q_BLOCK * DILATED_PATTERNS[-1][1]
    assert s % tile == 0
    n_pat = len(DILATED_PATTERNS)
    tiled = lambda part: pl.BlockSpec((None, 1, tile, HEAD_DIM), lambda bi, h, t: (bi, part * n_heads + h, t, 0))
    stats = pltpu.VMEM((n_pat, tile, LANES), _F32)
    return pl.pallas_call(
        functools.partial(_dilated_kernel, tile),
        grid=(b, n_heads, s // tile),
        in_specs=[tiled(0), tiled(1), tiled(2)],
        out_specs=pl.BlockSpec((1, tile, HEAD_DIM), lambda bi, h, t: (bi, t, h)),
        out_shape=jax.ShapeDtypeStruct((b, s, n_heads * HEAD_DIM), _BF16),
        scratch_shapes=[pltpu.VMEM((n_pat, tile, HEAD_DIM), _BF16),
                        pltpu.VMEM((2, n_pat, tile, HEAD_DIM), _BF16),
                        pltpu.VMEM((2, n_pat, tile, HEAD_DIM), _BF16),
                        stats, stats, pltpu.VMEM((tile, HEAD_DIM), _F32)],
        compiler_params=_params(52, 3),
        name="dilated_attention",
    )(qkv, qkv, qkv)


ONES_ROWS = 16
FOX_KEY_BLOCK = 512
FOX_QUERY_CHUNK = 512


def _fox_kernel(tq, q_ref, k_ref, v_ref, kx_ref, o_ref, vt_ref, m_ref, acc_ref, sa_ref, sb_ref):
    qi = pl.program_id(2)
    seq = v_ref.shape[1]
    tk = FOX_KEY_BLOCK
    n_sub = tq // tk

    @pl.when(qi == 0)
    def _():
        for c in range(seq // tk):
            blk = v_ref[0, c * tk:(c + 1) * tk, :].astype(_F32)
            vt_ref[:HEAD_DIM, c * tk:(c + 1) * tk] = blk.T.astype(_BF16)
        vt_ref[HEAD_DIM:, :] = jnp.ones((ONES_ROWS, seq), _BF16)

    depth = lax.broadcasted_iota(jnp.int32, (LANES, tq), 0)
    first = BIAS_PIECES * pl.program_id(1)
    ones = ((depth >= first) & (depth < first + BIAS_PIECES)).astype(_BF16)
    q_aug = jnp.concatenate([q_ref[0].astype(_F32).T.astype(_BF16), ones], axis=0)
    m_ref[...] = jnp.full_like(m_ref, NEG_INF)
    acc_ref[...] = jnp.zeros_like(acc_ref)

    def scores(j, s_ref, q0=0):
        k0 = pl.multiple_of(j * tk, tk)
        k_aug = jnp.concatenate([k_ref[0, pl.ds(k0, tk), :], kx_ref[0, pl.ds(k0, tk), :]], axis=1)
        s_ref[:, :tq - q0] = jnp.dot(k_aug, q_aug[:, q0:], preferred_element_type=_F32)

    def update(j, s_ref, q0=0, diagonal=False):
        cw = FOX_QUERY_CHUNK
        vt = vt_ref[:, pl.ds(pl.multiple_of(j * tk, tk), tk)]
        for c in range((tq - q0) // cw):
            s = s_ref[:, c * cw:(c + 1) * cw]
            if diagonal and c * cw < tk:
                key = lax.broadcasted_iota(jnp.int32, s.shape, 0)
                qry = lax.broadcasted_iota(jnp.int32, s.shape, 1) + c * cw
                s = jnp.where(key <= qry, s, NEG_INF)
            cols = slice(q0 + c * cw, q0 + (c + 1) * cw)
            m_prev = m_ref[:, cols]
            m_new = jnp.maximum(m_prev, jnp.max(s, axis=0, keepdims=True))
            a = jnp.exp2(m_prev - m_new)
            p = jnp.exp2(s - m_new).astype(_BF16)
            acc_ref[:, cols] = a * acc_ref[:, cols] + jnp.dot(vt, p, preferred_element_type=_F32)
            m_ref[:, cols] = m_new

    def pair(j0):
        scores(j0 + 1, sb_ref)
        update(j0, sa_ref)
        scores(j0 + 2, sa_ref)
        update(j0 + 1, sb_ref)

    def below_diagonal(jj, carry):
        for u in range(0, n_sub, 2):
            pair(n_sub * jj + u)
        return carry

    scores(0, sa_ref)
    lax.fori_loop(0, qi, below_diagonal, 0)
    bufs = (sa_ref, sb_ref)
    for d in range(n_sub):
        if d + 1 < n_sub:
            scores(n_sub * qi + d + 1, bufs[(d + 1) % 2], q0=(d + 1) * tk)
        update(n_sub * qi + d, bufs[d % 2], q0=d * tk, diagonal=True)

    out_t = acc_ref[:HEAD_DIM, :] / acc_ref[HEAD_DIM:HEAD_DIM + 1, :]
    o_ref[0] = out_t.T.astype(o_ref.dtype)


def _forgetting_attention(qkv, kx, n_heads):
    b, _, s, _ = qkv.shape
    tq = _pick(s, (4 * FOX_KEY_BLOCK, 2 * FOX_KEY_BLOCK))
    whole = lambda part: pl.BlockSpec((None, 1, s, HEAD_DIM), lambda bi, h, t: (bi, part * n_heads + h, 0, 0))
    return pl.pallas_call(
        functools.partial(_fox_kernel, tq),
        grid=(b, n_heads, s // tq),
        in_specs=[pl.BlockSpec((None, 1, tq, HEAD_DIM), lambda bi, h, t: (bi, h, t, 0)), whole(1), whole(2),
                  pl.BlockSpec((1, s, LANES), lambda bi, h, t: (bi, 0, 0))],
        out_specs=pl.BlockSpec((1, tq, HEAD_DIM), lambda bi, h, t: (bi, t, h)),
        out_shape=jax.ShapeDtypeStruct((b, s, n_heads * HEAD_DIM), _BF16),
        scratch_shapes=[pltpu.VMEM((HEAD_DIM + ONES_ROWS, s), _BF16),
                        pltpu.VMEM((1, tq), _F32),
                        pltpu.VMEM((HEAD_DIM + ONES_ROWS, tq), _F32),
                        pltpu.VMEM((FOX_KEY_BLOCK, tq), _F32),
                        pltpu.VMEM((FOX_KEY_BLOCK, tq), _F32)],
        compiler_params=_params(48, 3),
        name="forgetting_attention",
    )(qkv, qkv, qkv, kx)


def _merge_kernel(alpha, d_model, oa_ref, za_ref, ob_ref, zb_ref, g_ref, x_ref,
                  wa_ref, wb_ref, wo_ref, lng_ref, lnb_ref, o_ref, ob16_ref):
    up_a = jnp.dot(oa_ref[...] * za_ref[...], wa_ref[...], preferred_element_type=_F32)
    up_b = jnp.dot(ob_ref[...] * zb_ref[...], wb_ref[...], preferred_element_type=_F32)
    mix = (g_ref[:, :d_model].astype(_F32) * up_a + g_ref[:, d_model:].astype(_F32) * up_b)
    y = jnp.dot(mix.astype(_BF16), wo_ref[...], preferred_element_type=_F32)
    r = alpha * x_ref[...] + y
    mu = jnp.mean(r, axis=-1, keepdims=True)
    cen = r - mu
    var = jnp.mean(cen * cen, axis=-1, keepdims=True)
    out = cen * lax.rsqrt(var + LN_EPS) * lng_ref[...] + lnb_ref[...]
    o_ref[...] = out
    ob16_ref[...] = out.astype(_BF16)


def _merge(oa, sza, ob, szb, gates, x, w_up_a, w_up_b, w_out, ln_g, ln_b, alpha):
    m, d_model = x.shape
    w = oa.shape[1]
    tm = _pick(m, (256, 128))
    row = lambda n: pl.BlockSpec((tm, n), lambda i: (i, 0))
    const = lambda a, c: pl.BlockSpec((a, c), lambda i: (0, 0), pipeline_mode=pl.Buffered(1))
    return pl.pallas_call(
        functools.partial(_merge_kernel, alpha, d_model),
        grid=(m // tm,),
        in_specs=[row(w), row(w), row(w), row(w), row(2 * d_model), row(d_model),
                  const(w, d_model), const(w, d_model), const(d_model, d_model),
                  const(1, d_model), const(1, d_model)],
        out_specs=[row(d_model), row(d_model)],
        out_shape=[jax.ShapeDtypeStruct((m, d_model), _F32),
                   jax.ShapeDtypeStruct((m, d_model), _BF16)],
        compiler_params=_params(48, 1),
        name="merge_deepnorm",
    )(oa, sza, ob, szb, gates, x, w_up_a, w_up_b, w_out, ln_g, ln_b)


def _rope_tables(seq):
    half = HEAD_DIM // 2
    inv_freq = ROPE_THETA ** (-jnp.arange(half, dtype=_F32) / half)
    ang = jnp.arange(seq, dtype=_F32)[:, None] * inv_freq[None, :]
    cos, sin = jnp.cos(ang), jnp.sin(ang)
    return jnp.concatenate([cos, cos], axis=1), jnp.concatenate([-sin, sin], axis=1)


def _layer(layer, x, xb, tables, w_in, b_forget, b_gate, w_up_a, w_up_b, w_out, ln_g, ln_b, dims):
    batch, seq, d_model, width_a, width_b, alpha = dims
    heads_a, heads_b = width_a // HEAD_DIM, width_b // HEAD_DIM
    m = batch * seq
    q_scale = LOG2E * HEAD_DIM ** -0.5
    o_za, o_qb = 3 * width_a, 4 * width_a
    o_zb, o_f = o_qb + 3 * width_b, o_qb + 4 * width_b
    o_g = o_f + heads_b
    ones = lambda n: jnp.ones((1, n), _F32)
    project = functools.partial(_project, xb=xb, w=w_in, layer=layer, seq=seq)

    scale_a = jnp.concatenate([jnp.full((1, width_a), q_scale, _F32), ones(2 * width_a)], axis=1)
    qkv_a = project("rope", col0=0, n=3 * width_a, extras=(*tables, scale_a), out_dtype=_F32,
                    rope_cols=2 * width_a)
    sz_a = project("silu", col0=o_za, n=width_a, extras=(), out_dtype=_BF16)
    out_a = _dilated_attention(qkv_a, heads_a)

    scale_b = jnp.concatenate([jnp.full((1, width_b), q_scale, _F32), ones(2 * width_b)], axis=1)
    qkv_b = project("linear", col0=o_qb, n=3 * width_b, extras=(scale_b,), out_dtype=_BF16)
    sz_b = project("silu", col0=o_zb, n=width_b, extras=(), out_dtype=_BF16)
    w_f = jnp.pad(w_in[layer, :, o_f:o_g], ((0, 0), (0, LANES - heads_b)))
    b_f = jnp.pad(b_forget, (0, LANES - heads_b))[None, :]
    kx = _forget_bias(xb, w_f, b_f, batch, seq, heads_b)
    out_b = _forgetting_attention(qkv_b, kx.reshape(batch, seq, LANES), heads_b)

    w_g = w_in[layer, :, o_g:o_g + 2 * d_model][None]
    gates = _project("gate", xb, w_g, 0, 0, 2 * d_model, (b_gate[None, :],), _BF16, seq)
    return _merge(out_a.reshape(m, width_a), sz_a, out_b.reshape(m, width_b), sz_b, gates, x,
                  w_up_a.astype(_BF16), w_up_b.astype(_BF16), w_out.astype(_BF16),
                  ln_g[None, :], ln_b[None, :], alpha)


def kernel(x, w_in, b_forget, b_gate, w_up_a, w_up_b, w_out, ln_g, ln_b):
    batch, seq, d_model = x.shape
    depth = w_in.shape[0]
    width_a, width_b = w_up_a.shape[1], w_up_b.shape[1]
    alpha = float((2 * depth) ** 0.25)
    dims = (batch, seq, d_model, width_a, width_b, alpha)
    tables = _rope_tables(seq)
    x2 = x.reshape(batch * seq, d_model)
    xb = x2.astype(_BF16)
    for l in range(depth):
        x2, xb = _layer(l, x2, xb, tables, w_in, b_forget[l], b_gate[l],
                        w_up_a[l], w_up_b[l], w_out[l], ln_g[l], ln_b[l], dims)
    return x2.reshape(batch, seq, d_model)
```

```python
import functools
import math

import jax
import jax.numpy as jnp
from jax import lax
from jax.experimental import pallas as pl
from jax.experimental.pallas import tpu as pltpu

HEAD_DIM = 128
Q_BLOCK = 128
DILATED_PATTERNS = ((128, 1), (512, 4), (2048, 16))
ROPE_THETA = 10000.0
LN_EPS = 1e-5
LANES = 128
NEG_INF = float("-inf")
LOG2E = math.log2(math.e)
BIAS_PIECES = 3

_F32 = jnp.float32
_BF16 = jnp.bfloat16
_NT = (((1,), (1,)), ((), ()))


def _pick(n, candidates):
    for c in candidates:
        if n % c == 0:
            return c
    raise ValueError(f"no tile in {candidates} divides {n}")


def _params(vmem_mib, n_axes):
    return pltpu.CompilerParams(dimension_semantics=("arbitrary",) * n_axes,
                                vmem_limit_bytes=vmem_mib * 1024 * 1024)


def _lane_chunks(a):
    return [a[:, c * LANES:(c + 1) * LANES] for c in range(a.shape[1] // LANES)]


def _split_bf16(a):
    hi = a.astype(_BF16).astype(_F32)
    mid = (a - hi).astype(_BF16).astype(_F32)
    return hi, mid, a - hi - mid


PROJ_SUB_ROWS = 256


def _proj_kernel(kind, rope_tiles, x_ref, w_ref, *rest):
    o_ref = rest[-1]
    w = w_ref[0].astype(_BF16)
    if kind == "rope":
        cos_ref, sin_ref, scale_ref = rest[:3]
        on = jnp.where(pl.program_id(1) < rope_tiles, 1.0, 0.0)
    def put_heads(rows, c, val):
        o_ref[c, rows, :] = val.astype(o_ref.dtype)

    for rb in range(x_ref.shape[0] // PROJ_SUB_ROWS):
        rows = slice(rb * PROJ_SUB_ROWS, (rb + 1) * PROJ_SUB_ROWS)
        acc = jnp.dot(x_ref[rows, :], w, preferred_element_type=_F32)
        if kind == "rope":
            cos = 1.0 + on * (cos_ref[rows, :] - 1.0)
            sin = on * sin_ref[rows, :]
            for c, t in enumerate(_lane_chunks(acc)):
                r = t * cos + pltpu.roll(t, HEAD_DIM // 2, axis=1) * sin
                put_heads(rows, c, r * scale_ref[:, c * LANES:(c + 1) * LANES])
        elif kind == "linear":
            for c, t in enumerate(_lane_chunks(acc * rest[0][...])):
                put_heads(rows, c, t)
        elif kind == "silu":
            o_ref[rows, :] = (acc * jax.nn.sigmoid(acc)).astype(o_ref.dtype)
        elif kind == "gate":
            o_ref[rows, :] = jax.nn.sigmoid(acc + rest[0][...]).astype(o_ref.dtype)
        else:
            raise ValueError(kind)


def _project(kind, xb, w, layer, col0, n, extras, out_dtype, seq, rope_cols=0):
    m, k = xb.shape
    tm = _pick(seq, (2048, 1024, 512, 256))
    tn = _pick(math.gcd(math.gcd(n, rope_cols), col0), (1024, 768, 512, 256, 128))
    j0 = col0 // tn
    nrow = seq // tm
    if kind in ("rope", "linear"):
        out_spec = pl.BlockSpec((None, tn // LANES, tm, LANES), lambda i, j: (i // nrow, j, i % nrow, 0))
        out_shape = jax.ShapeDtypeStruct((m // seq, n // LANES, seq, LANES), out_dtype)
    else:
        out_spec = pl.BlockSpec((tm, tn), lambda i, j: (i, j))
        out_shape = jax.ShapeDtypeStruct((m, n), out_dtype)
    in_specs = [pl.BlockSpec((tm, k), lambda i, j: (i, 0)),
                pl.BlockSpec((1, k, tn), lambda i, j: (layer, 0, j0 + j))]
    if kind == "rope":
        table = pl.BlockSpec((tm, HEAD_DIM), lambda i, j: (i % nrow, 0))
        in_specs += [table, table, pl.BlockSpec((1, tn), lambda i, j: (0, j))]
    else:
        in_specs += [pl.BlockSpec((1, tn), lambda i, j: (0, j)) for _ in extras]
    return pl.pallas_call(
        functools.partial(_proj_kernel, kind, rope_cols // tn),
        grid=(m // tm, n // tn),
        in_specs=in_specs,
        out_specs=out_spec,
        out_shape=out_shape,
        compiler_params=_params(56, 2),
        name=f"proj_{kind}",
    )(xb, w, *extras)


def _forget_kernel(tiles_per_seq, n_heads, x_ref, w_ref, b_ref, tri_ref, o_ref, carry_ref):
    @pl.when(pl.program_id(0) % tiles_per_seq == 0)
    def _():
        carry_ref[...] = jnp.zeros_like(carry_ref)

    z = jnp.dot(x_ref[...], w_ref[...].astype(_BF16), preferred_element_type=_F32) + b_ref[...]
    log_f = jnp.minimum(z, 0.0) - jnp.log1p(jnp.exp(-jnp.abs(z)))
    tri = tri_ref[...]
    cum = carry_ref[...]
    for piece in _split_bf16(log_f):
        cum = cum + jnp.dot(tri, piece.astype(_BF16), preferred_element_type=_F32)
    carry_ref[...] = cum[-1:, :]
    head = lax.broadcasted_iota(jnp.int32, (LANES, LANES), 0)
    lane = lax.broadcasted_iota(jnp.int32, (LANES, LANES), 1)
    out = jnp.zeros(cum.shape, _F32)
    for k, piece in enumerate(_split_bf16(cum * (-LOG2E))):
        place = ((lane == BIAS_PIECES * head + k) & (head < n_heads)).astype(_BF16)
        out = out + jnp.dot(piece.astype(_BF16), place, preferred_element_type=_F32)
    o_ref[...] = out.astype(_BF16)


def _forget_bias(xb, w_f, b_f, batch, seq, n_heads):
    assert BIAS_PIECES * n_heads <= LANES
    m, k = xb.shape
    tm = _pick(seq, (512, 256, 128))
    tiles = seq // tm
    tri = (lax.broadcasted_iota(jnp.int32, (tm, tm), 0)
           >= lax.broadcasted_iota(jnp.int32, (tm, tm), 1)).astype(_BF16)
    return pl.pallas_call(
        functools.partial(_forget_kernel, tiles, n_heads),
        grid=(m // tm,),
        in_specs=[pl.BlockSpec((tm, k), lambda i: (i, 0)),
                  pl.BlockSpec((k, LANES), lambda i: (0, 0)),
                  pl.BlockSpec((1, LANES), lambda i: (0, 0)),
                  pl.BlockSpec((tm, tm), lambda i: (0, 0))],
        out_specs=pl.BlockSpec((tm, LANES), lambda i: (i, 0)),
        out_shape=jax.ShapeDtypeStruct((m, LANES), _BF16),
        scratch_shapes=[pltpu.VMEM((1, LANES), _F32)],
        compiler_params=_params(32, 1),
        name="forget_bias",
    )(xb, w_f, b_f, tri)


def _dilated_kernel(tile, q_ref, k_ref, v_ref, o_ref, qs_ref, ks_ref, vs_ref, out_ref, lse_ref, mid_ref):
    t = pl.program_id(2)
    slot = t % 2
    row = lax.broadcasted_iota(jnp.int32, (Q_BLOCK, LANES), 0)
    lane = lax.broadcasted_iota(jnp.int32, (Q_BLOCK, LANES), 1)
    in_prev = lane >= row
    in_cur = lane <= row
    ones = jnp.ones((2 * Q_BLOCK, LANES), _BF16)

    @pl.when(t == 0)
    def _():
        vs_ref[1] = jnp.zeros(vs_ref.shape[1:], _BF16)

    (_, d1), (_, d2), (_, d3) = DILATED_PATTERNS
    assert d1 == 1 and d3 % d2 == 0
    per2, per3, step3 = tile // d2, tile // d3, d3 // d2
    for src_ref, dst_ref in ((q_ref, qs_ref), (k_ref, ks_ref.at[slot]), (v_ref, vs_ref.at[slot])):
        dst_ref[0] = src_ref[0].astype(_BF16)
        for r2 in range(d2):
            cls = src_ref[0, pl.ds(r2, per2, stride=d2), :]
            mid_ref[pl.ds(r2 * per2, per2), :] = cls
            dst_ref[1, pl.ds(r2 * per2, per2), :] = cls.astype(_BF16)
        for r3 in range(d3):
            cls = mid_ref[pl.ds((r3 % d2) * per2 + r3 // d2, per3, stride=step3), :]
            dst_ref[2, pl.ds(r3 * per3, per3), :] = cls.astype(_BF16)

    for g, (_, d) in enumerate(DILATED_PATTERNS):
        per = tile // d
        for r in range(d):
            for n in range(per // Q_BLOCK):
                base = r * per + n * Q_BLOCK
                q = qs_ref[g, pl.ds(base, Q_BLOCK), :]
                if n > 0:
                    kk = ks_ref[slot, g, pl.ds(base - Q_BLOCK, 2 * Q_BLOCK), :]
                    vv = vs_ref[slot, g, pl.ds(base - Q_BLOCK, 2 * Q_BLOCK), :]
                    prev_ok = in_prev
                else:
                    last = pl.ds((r + 1) * per - Q_BLOCK, Q_BLOCK)
                    kk = jnp.concatenate([ks_ref[1 - slot, g, last, :], ks_ref[slot, g, pl.ds(base, Q_BLOCK), :]], axis=0)
                    vv = jnp.concatenate([vs_ref[1 - slot, g, last, :], vs_ref[slot, g, pl.ds(base, Q_BLOCK), :]], axis=0)
                    prev_ok = in_prev & (t > 0)
                s_prev, s_cur = _lane_chunks(lax.dot_general(q, kk, _NT, preferred_element_type=_F32))
                s_prev = jnp.where(prev_ok, s_prev, NEG_INF)
                s_cur = jnp.where(in_cur, s_cur, NEG_INF)
                m = jnp.broadcast_to(jnp.max(jnp.maximum(s_prev, s_cur), axis=1, keepdims=True), s_cur.shape)
                p = jnp.concatenate([jnp.exp2(s_prev - m), jnp.exp2(s_cur - m)], axis=1).astype(_BF16)
                acc, l = _lane_chunks(jnp.dot(p, jnp.concatenate([vv, ones], axis=1), preferred_element_type=_F32))
                nat = pl.ds(n * Q_BLOCK * d + r, Q_BLOCK) if d == 1 else pl.ds(n * Q_BLOCK * d + r, Q_BLOCK, stride=d)
                out_ref[g, nat, :] = acc / l
                lse_ref[g, nat, :] = m + jnp.log2(l)

    lse_all = jnp.maximum(jnp.maximum(lse_ref[0], lse_ref[1]), lse_ref[2])
    num = jnp.zeros((tile, HEAD_DIM), _F32)
    den = jnp.zeros((tile, LANES), _F32)
    for g in range(len(DILATED_PATTERNS)):
        e = jnp.exp2(lse_ref[g] - lse_all)
        num = num + e * out_ref[g]
        den = den + e
    o_ref[0] = (num / den).astype(o_ref.dtype)


def _dilated_attention(qkv, n_heads):
    b, _, s, _ = qkv.shape
    tile = Q_BLOCK * DILATED_PATTERNS[-1][1]
    assert s % tile == 0
    n_pat = len(DILATED_PATTERNS)
    tiled = lambda part: pl.BlockSpec((None, 1, tile, HEAD_DIM), lambda bi, h, t: (bi, part * n_heads + h, t, 0))
    stats = pltpu.VMEM((n_pat, tile, LANES), _F32)
    return pl.pallas_call(
        functools.partial(_dilated_kernel, tile),
        grid=(b, n_heads, s // tile),
        in_specs=[tiled(0), tiled(1), tiled(2)],
        out_specs=pl.BlockSpec((1, tile, HEAD_DIM), lambda bi, h, t: (bi, t, h)),
        out_shape=jax.ShapeDtypeStruct((b, s, n_heads * HEAD_DIM), _BF16),
        scratch_shapes=[pltpu.VMEM((n_pat, tile, HEAD_DIM), _BF16),
                        pltpu.VMEM((2, n_pat, tile, HEAD_DIM), _BF16),
                        pltpu.VMEM((2, n_pat, tile, HEAD_DIM), _BF16),
                        stats, stats, pltpu.VMEM((tile, HEAD_DIM), _F32)],
        compiler_params=_params(40, 3),
        name="dilated_attention",
    )(qkv, qkv, qkv)


ONES_ROWS = 16
FOX_KEY_BLOCK = 512


def _fox_kernel(tq, q_ref, k_ref, v_ref, kx_ref, o_ref, vt_ref, m_ref, acc_ref, sa_ref, sb_ref):
    qi = pl.program_id(2)
    seq = v_ref.shape[1]
    tk = FOX_KEY_BLOCK
    n_sub = tq // tk

    @pl.when(qi == 0)
    def _():
        for c in range(seq // tk):
            blk = v_ref[0, c * tk:(c + 1) * tk, :].astype(_F32)
            vt_ref[:HEAD_DIM, c * tk:(c + 1) * tk] = blk.T.astype(_BF16)
        vt_ref[HEAD_DIM:, :] = jnp.ones((ONES_ROWS, seq), _BF16)

    depth = lax.broadcasted_iota(jnp.int32, (LANES, tq), 0)
    first = BIAS_PIECES * pl.program_id(1)
    ones = ((depth >= first) & (depth < first + BIAS_PIECES)).astype(_BF16)
    q_aug = jnp.concatenate([q_ref[0].astype(_F32).T.astype(_BF16), ones], axis=0)
    m_ref[...] = jnp.full_like(m_ref, NEG_INF)
    acc_ref[...] = jnp.zeros_like(acc_ref)

    def k_block(j):
        k0 = pl.multiple_of(j * tk, tk)
        return jnp.concatenate([k_ref[0, pl.ds(k0, tk), :], kx_ref[0, pl.ds(k0, tk), :]], axis=1)

    def scores_chunk(k_aug, s_ref, q0, c):
        cols = slice(q0 + c * tk, q0 + (c + 1) * tk)
        s_ref[:, c * tk:(c + 1) * tk] = jnp.dot(k_aug, q_aug[:, cols], preferred_element_type=_F32)

    def update_chunk(vt, s_ref, q0, c, diagonal):
        s = s_ref[:, c * tk:(c + 1) * tk]
        if diagonal and c == 0:
            key = lax.broadcasted_iota(jnp.int32, s.shape, 0)
            qry = lax.broadcasted_iota(jnp.int32, s.shape, 1)
            s = jnp.where(key <= qry, s, NEG_INF)
        cols = slice(q0 + c * tk, q0 + (c + 1) * tk)
        m_prev = m_ref[:, cols]
        m_new = jnp.maximum(m_prev, jnp.max(s, axis=0, keepdims=True))
        a = jnp.exp2(m_prev - m_new)
        p = jnp.exp2(s - m_new).astype(_BF16)
        acc_ref[:, cols] = a * acc_ref[:, cols] + jnp.dot(vt, p, preferred_element_type=_F32)
        m_ref[:, cols] = m_new

    def step(j, s_cur, s_next, q0=0, diagonal=False, q0_next=None):
        vt = vt_ref[:, pl.ds(pl.multiple_of(j * tk, tk), tk)]
        n_cur = (tq - q0) // tk
        n_next = 0 if q0_next is None else (tq - q0_next) // tk
        k_next = k_block(j + 1) if n_next else None
        for c in range(max(n_cur, n_next)):
            if c < n_next:
                scores_chunk(k_next, s_next, q0_next, c)
            if c < n_cur:
                update_chunk(vt, s_cur, q0, c, diagonal)

    def buffers(u):
        return (sa_ref, sb_ref) if u % 2 == 0 else (sb_ref, sa_ref)

    def below_diagonal(jj, carry):
        for u in range(n_sub):
            step(n_sub * jj + u, *buffers(u), q0_next=0)
        return carry

    first_keys = k_block(0)
    for c in range(n_sub):
        scores_chunk(first_keys, sa_ref, 0, c)
    lax.fori_loop(0, qi, below_diagonal, 0)
    for d in range(n_sub):
        step(n_sub * qi + d, *buffers(d), q0=d * tk, diagonal=True,
             q0_next=(d + 1) * tk if d + 1 < n_sub else None)

    out_t = acc_ref[:HEAD_DIM, :] / acc_ref[HEAD_DIM:HEAD_DIM + 1, :]
    o_ref[0] = out_t.T.astype(o_ref.dtype)


def _forgetting_attention(qkv, kx, n_heads):
    b, _, s, _ = qkv.shape
    tq = _pick(s, (4 * FOX_KEY_BLOCK, 2 * FOX_KEY_BLOCK))
    whole = lambda part: pl.BlockSpec((None, 1, s, HEAD_DIM), lambda bi, h, t: (bi, part * n_heads + h, 0, 0))
    return pl.pallas_call(
        functools.partial(_fox_kernel, tq),
        grid=(b, n_heads, s // tq),
        in_specs=[pl.BlockSpec((None, 1, tq, HEAD_DIM), lambda bi, h, t: (bi, h, t, 0)), whole(1), whole(2),
                  pl.BlockSpec((1, s, LANES), lambda bi, h, t: (bi, 0, 0))],
        out_specs=pl.BlockSpec((1, tq, HEAD_DIM), lambda bi, h, t: (bi, t, h)),
        out_shape=jax.ShapeDtypeStruct((b, s, n_heads * HEAD_DIM), _BF16),
        scratch_shapes=[pltpu.VMEM((HEAD_DIM + ONES_ROWS, s), _BF16),
                        pltpu.VMEM((1, tq), _F32),
                        pltpu.VMEM((HEAD_DIM + ONES_ROWS, tq), _F32),
                        pltpu.VMEM((FOX_KEY_BLOCK, tq), _F32),
                        pltpu.VMEM((FOX_KEY_BLOCK, tq), _F32)],
        compiler_params=_params(48, 3),
        name="forgetting_attention",
    )(qkv, qkv, qkv, kx)


def _merge_kernel(alpha, d_model, oa_ref, za_ref, ob_ref, zb_ref, g_ref, x_ref,
                  wa_ref, wb_ref, wo_ref, lng_ref, lnb_ref, o_ref, ob16_ref):
    up_a = jnp.dot(oa_ref[...] * za_ref[...], wa_ref[...], preferred_element_type=_F32)
    up_b = jnp.dot(ob_ref[...] * zb_ref[...], wb_ref[...], preferred_element_type=_F32)
    mix = (g_ref[:, :d_model].astype(_F32) * up_a + g_ref[:, d_model:].astype(_F32) * up_b)
    y = jnp.dot(mix.astype(_BF16), wo_ref[...], preferred_element_type=_F32)
    r = alpha * x_ref[...] + y
    mu = jnp.mean(r, axis=-1, keepdims=True)
    cen = r - mu
    var = jnp.mean(cen * cen, axis=-1, keepdims=True)
    out = cen * lax.rsqrt(var + LN_EPS) * lng_ref[...] + lnb_ref[...]
    o_ref[...] = out
    ob16_ref[...] = out.astype(_BF16)


def _merge(oa, sza, ob, szb, gates, x, w_up_a, w_up_b, w_out, ln_g, ln_b, alpha):
    m, d_model = x.shape
    w = oa.shape[1]
    tm = _pick(m, (256, 128))
    row = lambda n: pl.BlockSpec((tm, n), lambda i: (i, 0))
    const = lambda a, c: pl.BlockSpec((a, c), lambda i: (0, 0), pipeline_mode=pl.Buffered(1))
    return pl.pallas_call(
        functools.partial(_merge_kernel, alpha, d_model),
        grid=(m // tm,),
        in_specs=[row(w), row(w), row(w), row(w), row(2 * d_model), row(d_model),
                  const(w, d_model), const(w, d_model), const(d_model, d_model),
                  const(1, d_model), const(1, d_model)],
        out_specs=[row(d_model), row(d_model)],
        out_shape=[jax.ShapeDtypeStruct((m, d_model), _F32),
                   jax.ShapeDtypeStruct((m, d_model), _BF16)],
        compiler_params=_params(48, 1),
        name="merge_deepnorm",
    )(oa, sza, ob, szb, gates, x, w_up_a, w_up_b, w_out, ln_g, ln_b)


def _rope_tables(seq):
    half = HEAD_DIM // 2
    inv_freq = ROPE_THETA ** (-jnp.arange(half, dtype=_F32) / half)
    ang = jnp.arange(seq, dtype=_F32)[:, None] * inv_freq[None, :]
    cos, sin = jnp.cos(ang), jnp.sin(ang)
    return jnp.concatenate([cos, cos], axis=1), jnp.concatenate([-sin, sin], axis=1)


def _layer(layer, x, xb, tables, w_in, b_forget, b_gate, w_up_a, w_up_b, w_out, ln_g, ln_b, dims):
    batch, seq, d_model, width_a, width_b, alpha = dims
    heads_a, heads_b = width_a // HEAD_DIM, width_b // HEAD_DIM
    m = batch * seq
    q_scale = LOG2E * HEAD_DIM ** -0.5
    o_za, o_qb = 3 * width_a, 4 * width_a
    o_zb, o_f = o_qb + 3 * width_b, o_qb + 4 * width_b
    o_g = o_f + heads_b
    ones = lambda n: jnp.ones((1, n), _F32)
    project = functools.partial(_project, xb=xb, w=w_in, layer=layer, seq=seq)

    scale_a = jnp.concatenate([jnp.full((1, width_a), q_scale, _F32), ones(2 * width_a)], axis=1)
    qkv_a = project("rope", col0=0, n=3 * width_a, extras=(*tables, scale_a), out_dtype=_F32,
                    rope_cols=2 * width_a)
    sz_a = project("silu", col0=o_za, n=width_a, extras=(), out_dtype=_BF16)
    out_a = _dilated_attention(qkv_a, heads_a)

    scale_b = jnp.concatenate([jnp.full((1, width_b), q_scale, _F32), ones(2 * width_b)], axis=1)
    qkv_b = project("linear", col0=o_qb, n=3 * width_b, extras=(scale_b,), out_dtype=_BF16)
    sz_b = project("silu", col0=o_zb, n=width_b, extras=(), out_dtype=_BF16)
    w_f = jnp.pad(w_in[layer, :, o_f:o_g], ((0, 0), (0, LANES - heads_b)))
    b_f = jnp.pad(b_forget, (0, LANES - heads_b))[None, :]
    kx = _forget_bias(xb, w_f, b_f, batch, seq, heads_b)
    out_b = _forgetting_attention(qkv_b, kx.reshape(batch, seq, LANES), heads_b)

    w_g = w_in[layer, :, o_g:o_g + 2 * d_model][None]
    gates = _project("gate", xb, w_g, 0, 0, 2 * d_model, (b_gate[None, :],), _BF16, seq)
    return _merge(out_a.reshape(m, width_a), sz_a, out_b.reshape(m, width_b), sz_b, gates, x,
                  w_up_a.astype(_BF16), w_up_b.astype(_BF16), w_out.astype(_BF16),
                  ln_g[None, :], ln_b[None, :], alpha)


def kernel(x, w_in, b_forget, b_gate, w_up_a, w_up_b, w_out, ln_g, ln_b):
    batch, seq, d_model = x.shape
    depth = w_in.shape[0]
    width_a, width_b = w_up_a.shape[1], w_up_b.shape[1]
    alpha = float((2 * depth) ** 0.25)
    dims = (batch, seq, d_model, width_a, width_b, alpha)
    tables = _rope_tables(seq)
    x2 = x.reshape(batch * seq, d_model)
    xb = x2.astype(_BF16)
    for l in range(depth):
        x2, xb = _layer(l, x2, xb, tables, w_in, b_forget[l], b_gate[l],
                        w_up_a[l], w_up_b[l], w_out[l], ln_g[l], ln_b[l], dims)
    return x2.reshape(batch, seq, d_model)
```

```python
import functools
import math

import jax
import jax.numpy as jnp
from jax import lax
from jax.experimental import pallas as pl
from jax.experimental.pallas import tpu as pltpu

HEAD_DIM = 128
Q_BLOCK = 128
DILATED_PATTERNS = ((128, 1), (512, 4), (2048, 16))
ROPE_THETA = 10000.0
LN_EPS = 1e-5
LANES = 128
NEG_INF = float("-inf")
LOG2E = math.log2(math.e)
BIAS_PIECES = 3

_F32 = jnp.float32
_BF16 = jnp.bfloat16
_NT = (((1,), (1,)), ((), ()))


def _pick(n, candidates):
    for c in candidates:
        if n % c == 0:
            return c
    raise ValueError(f"no tile in {candidates} divides {n}")


def _params(vmem_mib, n_axes):
    return pltpu.CompilerParams(dimension_semantics=("arbitrary",) * n_axes,
                                vmem_limit_bytes=vmem_mib * 1024 * 1024)


def _lane_chunks(a):
    return [a[:, c * LANES:(c + 1) * LANES] for c in range(a.shape[1] // LANES)]


def _split_bf16(a):
    hi = a.astype(_BF16).astype(_F32)
    mid = (a - hi).astype(_BF16).astype(_F32)
    return hi, mid, a - hi - mid


PROJ_SUB_ROWS = 256


def _proj_kernel(kind, rope_tiles, x_ref, w_ref, *rest):
    o_ref = rest[-1]
    w = w_ref[0].astype(_BF16)
    if kind == "rope":
        cos_ref, sin_ref, scale_ref = rest[:3]
        on = jnp.where(pl.program_id(1) < rope_tiles, 1.0, 0.0)
    def put_heads(rows, c, val):
        o_ref[c, rows, :] = val.astype(o_ref.dtype)

    for rb in range(x_ref.shape[0] // PROJ_SUB_ROWS):
        rows = slice(rb * PROJ_SUB_ROWS, (rb + 1) * PROJ_SUB_ROWS)
        acc = jnp.dot(x_ref[rows, :], w, preferred_element_type=_F32)
        if kind == "rope":
            cos = 1.0 + on * (cos_ref[rows, :] - 1.0)
            sin = on * sin_ref[rows, :]
            for c, t in enumerate(_lane_chunks(acc)):
                r = t * cos + pltpu.roll(t, HEAD_DIM // 2, axis=1) * sin
                put_heads(rows, c, r * scale_ref[:, c * LANES:(c + 1) * LANES])
        elif kind == "linear":
            for c, t in enumerate(_lane_chunks(acc * rest[0][...])):
                put_heads(rows, c, t)
        elif kind == "silu":
            o_ref[rows, :] = (acc * jax.nn.sigmoid(acc)).astype(o_ref.dtype)
        elif kind == "gate":
            o_ref[rows, :] = jax.nn.sigmoid(acc + rest[0][...]).astype(o_ref.dtype)
        else:
            raise ValueError(kind)


def _project(kind, xb, w, layer, col0, n, extras, out_dtype, seq, rope_cols=0):
    m, k = xb.shape
    tm = _pick(seq, (2048, 1024, 512, 256))
    tn = _pick(math.gcd(math.gcd(n, rope_cols), col0), (1024, 768, 512, 256, 128))
    j0 = col0 // tn
    nrow = seq // tm
    if kind in ("rope", "linear"):
        out_spec = pl.BlockSpec((None, tn // LANES, tm, LANES), lambda i, j: (i // nrow, j, i % nrow, 0))
        out_shape = jax.ShapeDtypeStruct((m // seq, n // LANES, seq, LANES), out_dtype)
    else:
        out_spec = pl.BlockSpec((tm, tn), lambda i, j: (i, j))
        out_shape = jax.ShapeDtypeStruct((m, n), out_dtype)
    in_specs = [pl.BlockSpec((tm, k), lambda i, j: (i, 0)),
                pl.BlockSpec((1, k, tn), lambda i, j: (layer, 0, j0 + j))]
    if kind == "rope":
        table = pl.BlockSpec((tm, HEAD_DIM), lambda i, j: (i % nrow, 0))
        in_specs += [table, table, pl.BlockSpec((1, tn), lambda i, j: (0, j))]
    else:
        in_specs += [pl.BlockSpec((1, tn), lambda i, j: (0, j)) for _ in extras]
    return pl.pallas_call(
        functools.partial(_proj_kernel, kind, rope_cols // tn),
        grid=(m // tm, n // tn),
        in_specs=in_specs,
        out_specs=out_spec,
        out_shape=out_shape,
        compiler_params=_params(56, 2),
        name=f"proj_{kind}",
    )(xb, w, *extras)


def _forget_kernel(tiles_per_seq, n_heads, x_ref, w_ref, b_ref, tri_ref, o_ref, carry_ref):
    @pl.when(pl.program_id(0) % tiles_per_seq == 0)
    def _():
        carry_ref[...] = jnp.zeros_like(carry_ref)

    z = jnp.dot(x_ref[...], w_ref[...].astype(_BF16), preferred_element_type=_F32) + b_ref[...]
    log_f = jnp.minimum(z, 0.0) - jnp.log1p(jnp.exp(-jnp.abs(z)))
    tri = tri_ref[...]
    cum = carry_ref[...]
    for piece in _split_bf16(log_f):
        cum = cum + jnp.dot(tri, piece.astype(_BF16), preferred_element_type=_F32)
    carry_ref[...] = cum[-1:, :]
    head = lax.broadcasted_iota(jnp.int32, (LANES, LANES), 0)
    lane = lax.broadcasted_iota(jnp.int32, (LANES, LANES), 1)
    out = jnp.zeros(cum.shape, _F32)
    for k, piece in enumerate(_split_bf16(cum * (-LOG2E))):
        place = ((lane == BIAS_PIECES * head + k) & (head < n_heads)).astype(_BF16)
        out = out + jnp.dot(piece.astype(_BF16), place, preferred_element_type=_F32)
    o_ref[...] = out.astype(_BF16)


def _forget_bias(xb, w_f, b_f, batch, seq, n_heads):
    assert BIAS_PIECES * n_heads <= LANES
    m, k = xb.shape
    tm = _pick(seq, (512, 256, 128))
    tiles = seq // tm
    tri = (lax.broadcasted_iota(jnp.int32, (tm, tm), 0)
           >= lax.broadcasted_iota(jnp.int32, (tm, tm), 1)).astype(_BF16)
    return pl.pallas_call(
        functools.partial(_forget_kernel, tiles, n_heads),
        grid=(m // tm,),
        in_specs=[pl.BlockSpec((tm, k), lambda i: (i, 0)),
                  pl.BlockSpec((k, LANES), lambda i: (0, 0)),
                  pl.BlockSpec((1, LANES), lambda i: (0, 0)),
                  pl.BlockSpec((tm, tm), lambda i: (0, 0))],
        out_specs=pl.BlockSpec((tm, LANES), lambda i: (i, 0)),
        out_shape=jax.ShapeDtypeStruct((m, LANES), _BF16),
        scratch_shapes=[pltpu.VMEM((1, LANES), _F32)],
        compiler_params=_params(32, 1),
        name="forget_bias",
    )(xb, w_f, b_f, tri)


def _dilated_kernel(tile, q_ref, k_ref, v_ref, o_ref, qs_ref, ks_ref, vs_ref, out_ref, lse_ref, mid_ref,
                    kh_ref, vh_ref):
    t = pl.program_id(2)
    row = lax.broadcasted_iota(jnp.int32, (Q_BLOCK, LANES), 0)
    lane = lax.broadcasted_iota(jnp.int32, (Q_BLOCK, LANES), 1)
    in_prev = lane >= row
    in_cur = lane <= row
    ones = jnp.ones((2 * Q_BLOCK, LANES), _BF16)

    @pl.when(t == 0)
    def _():
        kh_ref[...] = jnp.zeros(kh_ref.shape, _BF16)
        vh_ref[...] = jnp.zeros(vh_ref.shape, _BF16)

    @pl.when(t > 0)
    def _():
        for g, (_, d) in enumerate(DILATED_PATTERNS):
            per = tile // d
            for r in range(d):
                last = pl.ds((r + 1) * per - Q_BLOCK, Q_BLOCK)
                kh_ref[g, pl.ds(r * Q_BLOCK, Q_BLOCK), :] = ks_ref[g, last, :]
                vh_ref[g, pl.ds(r * Q_BLOCK, Q_BLOCK), :] = vs_ref[g, last, :]

    (_, d1), (_, d2), (_, d3) = DILATED_PATTERNS
    assert d1 == 1 and d3 % d2 == 0
    per2, per3, step3 = tile // d2, tile // d3, d3 // d2
    for src_ref, dst_ref in ((q_ref, qs_ref), (k_ref, ks_ref), (v_ref, vs_ref)):
        dst_ref[0] = src_ref[0].astype(_BF16)
        for r2 in range(d2):
            cls = src_ref[0, pl.ds(r2, per2, stride=d2), :]
            mid_ref[pl.ds(r2 * per2, per2), :] = cls
            dst_ref[1, pl.ds(r2 * per2, per2), :] = cls.astype(_BF16)
        for r3 in range(d3):
            cls = mid_ref[pl.ds((r3 % d2) * per2 + r3 // d2, per3, stride=step3), :]
            dst_ref[2, pl.ds(r3 * per3, per3), :] = cls.astype(_BF16)

    for g, (_, d) in enumerate(DILATED_PATTERNS):
        per = tile // d
        for r in range(d):
            for n in range(per // Q_BLOCK):
                base = r * per + n * Q_BLOCK
                q = qs_ref[g, pl.ds(base, Q_BLOCK), :]
                if n > 0:
                    kk = ks_ref[g, pl.ds(base - Q_BLOCK, 2 * Q_BLOCK), :]
                    vv = vs_ref[g, pl.ds(base - Q_BLOCK, 2 * Q_BLOCK), :]
                    prev_ok = in_prev
                else:
                    kept = pl.ds(r * Q_BLOCK, Q_BLOCK)
                    kk = jnp.concatenate([kh_ref[g, kept, :], ks_ref[g, pl.ds(base, Q_BLOCK), :]], axis=0)
                    vv = jnp.concatenate([vh_ref[g, kept, :], vs_ref[g, pl.ds(base, Q_BLOCK), :]], axis=0)
                    prev_ok = in_prev & (t > 0)
                s_prev, s_cur = _lane_chunks(lax.dot_general(q, kk, _NT, preferred_element_type=_F32))
                s_prev = jnp.where(prev_ok, s_prev, NEG_INF)
                s_cur = jnp.where(in_cur, s_cur, NEG_INF)
                m = jnp.broadcast_to(jnp.max(jnp.maximum(s_prev, s_cur), axis=1, keepdims=True), s_cur.shape)
                p = jnp.concatenate([jnp.exp2(s_prev - m), jnp.exp2(s_cur - m)], axis=1).astype(_BF16)
                acc, l = _lane_chunks(jnp.dot(p, jnp.concatenate([vv, ones], axis=1), preferred_element_type=_F32))
                nat = pl.ds(n * Q_BLOCK * d + r, Q_BLOCK) if d == 1 else pl.ds(n * Q_BLOCK * d + r, Q_BLOCK, stride=d)
                out_ref[g, nat, :] = acc / l
                lse_ref[g, nat, :] = m + jnp.log2(l)

    lse_all = jnp.maximum(jnp.maximum(lse_ref[0], lse_ref[1]), lse_ref[2])
    num = jnp.zeros((tile, HEAD_DIM), _F32)
    den = jnp.zeros((tile, LANES), _F32)
    for g in range(len(DILATED_PATTERNS)):
        e = jnp.exp2(lse_ref[g] - lse_all)
        num = num + e * out_ref[g]
        den = den + e
    o_ref[0] = (num / den).astype(o_ref.dtype)


def _dilated_attention(qkv, n_heads):
    b, _, s, _ = qkv.shape
    tile = Q_BLOCK * DILATED_PATTERNS[-1][1]
    assert s % tile == 0
    n_pat = len(DILATED_PATTERNS)
    tiled = lambda part: pl.BlockSpec((None, 1, tile, HEAD_DIM), lambda bi, h, t: (bi, part * n_heads + h, t, 0))
    stats = pltpu.VMEM((n_pat, tile, LANES), _F32)
    staged = pltpu.VMEM((n_pat, tile, HEAD_DIM), _BF16)
    kept = pltpu.VMEM((n_pat, Q_BLOCK * DILATED_PATTERNS[-1][1], HEAD_DIM), _BF16)
    return pl.pallas_call(
        functools.partial(_dilated_kernel, tile),
        grid=(b, n_heads, s // tile),
        in_specs=[tiled(0), tiled(1), tiled(2)],
        out_specs=pl.BlockSpec((1, tile, HEAD_DIM), lambda bi, h, t: (bi, t, h)),
        out_shape=jax.ShapeDtypeStruct((b, s, n_heads * HEAD_DIM), _BF16),
        scratch_shapes=[staged, staged, staged, stats, stats, pltpu.VMEM((tile, HEAD_DIM), _F32), kept, kept],
        compiler_params=_params(40, 3),
        name="dilated_attention",
    )(qkv, qkv, qkv)


ONES_ROWS = 16
FOX_KEY_BLOCK = 512


def _fox_kernel(tq, q_ref, k_ref, v_ref, kx_ref, o_ref, vt_ref, m_ref, acc_ref, sa_ref, sb_ref):
    qi = pl.program_id(2)
    seq = v_ref.shape[1]
    tk = FOX_KEY_BLOCK
    n_sub = tq // tk

    @pl.when(qi == 0)
    def _():
        for c in range(seq // tk):
            blk = v_ref[0, c * tk:(c + 1) * tk, :].astype(_F32)
            vt_ref[:HEAD_DIM, c * tk:(c + 1) * tk] = blk.T.astype(_BF16)
        vt_ref[HEAD_DIM:, :] = jnp.ones((ONES_ROWS, seq), _BF16)

    depth = lax.broadcasted_iota(jnp.int32, (LANES, tq), 0)
    first = BIAS_PIECES * pl.program_id(1)
    ones = ((depth >= first) & (depth < first + BIAS_PIECES)).astype(_BF16)
    q_aug = jnp.concatenate([q_ref[0].astype(_F32).T.astype(_BF16), ones], axis=0)
    m_ref[...] = jnp.full_like(m_ref, NEG_INF)
    acc_ref[...] = jnp.zeros_like(acc_ref)

    def k_block(j):
        k0 = pl.multiple_of(j * tk, tk)
        return jnp.concatenate([k_ref[0, pl.ds(k0, tk), :], kx_ref[0, pl.ds(k0, tk), :]], axis=1)

    def scores_chunk(k_aug, s_ref, q0, c):
        cols = slice(q0 + c * tk, q0 + (c + 1) * tk)
        s_ref[:, c * tk:(c + 1) * tk] = jnp.dot(k_aug, q_aug[:, cols], preferred_element_type=_F32)

    def update_chunk(vt, s_ref, q0, c, diagonal):
        s = s_ref[:, c * tk:(c + 1) * tk]
        if diagonal and c == 0:
            key = lax.broadcasted_iota(jnp.int32, s.shape, 0)
            qry = lax.broadcasted_iota(jnp.int32, s.shape, 1)
            s = jnp.where(key <= qry, s, NEG_INF)
        cols = slice(q0 + c * tk, q0 + (c + 1) * tk)
        m_prev = m_ref[:, cols]
        m_new = jnp.maximum(m_prev, jnp.max(s, axis=0, keepdims=True))
        a = jnp.exp2(m_prev - m_new)
        p = jnp.exp2(s - m_new).astype(_BF16)
        acc_ref[:, cols] = a * acc_ref[:, cols] + jnp.dot(vt, p, preferred_element_type=_F32)
        m_ref[:, cols] = m_new

    def step(j, s_cur, s_next, q0=0, diagonal=False, q0_next=None):
        vt = vt_ref[:, pl.ds(pl.multiple_of(j * tk, tk), tk)]
        n_cur = (tq - q0) // tk
        n_next = 0 if q0_next is None else (tq - q0_next) // tk
        k_next = k_block(j + 1) if n_next else None
        for c in range(max(n_cur, n_next)):
            if c < n_next:
                scores_chunk(k_next, s_next, q0_next, c)
            if c < n_cur:
                update_chunk(vt, s_cur, q0, c, diagonal)

    def buffers(u):
        return (sa_ref, sb_ref) if u % 2 == 0 else (sb_ref, sa_ref)

    def below_diagonal(jj, carry):
        for u in range(n_sub):
            step(n_sub * jj + u, *buffers(u), q0_next=0)
        return carry

    first_keys = k_block(0)
    for c in range(n_sub):
        scores_chunk(first_keys, sa_ref, 0, c)
    lax.fori_loop(0, qi, below_diagonal, 0)
    for d in range(n_sub):
        step(n_sub * qi + d, *buffers(d), q0=d * tk, diagonal=True,
             q0_next=(d + 1) * tk if d + 1 < n_sub else None)

    out_t = acc_ref[:HEAD_DIM, :] / acc_ref[HEAD_DIM:HEAD_DIM + 1, :]
    o_ref[0] = out_t.T.astype(o_ref.dtype)


def _forgetting_attention(qkv, kx, n_heads):
    b, _, s, _ = qkv.shape
    tq = _pick(s, (4 * FOX_KEY_BLOCK, 2 * FOX_KEY_BLOCK))
    whole = lambda part: pl.BlockSpec((None, 1, s, HEAD_DIM), lambda bi, h, t: (bi, part * n_heads + h, 0, 0))
    return pl.pallas_call(
        functools.partial(_fox_kernel, tq),
        grid=(b, n_heads, s // tq),
        in_specs=[pl.BlockSpec((None, 1, tq, HEAD_DIM), lambda bi, h, t: (bi, h, t, 0)), whole(1), whole(2),
                  pl.BlockSpec((1, s, LANES), lambda bi, h, t: (bi, 0, 0))],
        out_specs=pl.BlockSpec((1, tq, HEAD_DIM), lambda bi, h, t: (bi, t, h)),
        out_shape=jax.ShapeDtypeStruct((b, s, n_heads * HEAD_DIM), _BF16),
        scratch_shapes=[pltpu.VMEM((HEAD_DIM + ONES_ROWS, s), _BF16),
                        pltpu.VMEM((1, tq), _F32),
                        pltpu.VMEM((HEAD_DIM + ONES_ROWS, tq), _F32),
                        pltpu.VMEM((FOX_KEY_BLOCK, tq), _F32),
                        pltpu.VMEM((FOX_KEY_BLOCK, tq), _F32)],
        compiler_params=_params(48, 3),
        name="forgetting_attention",
    )(qkv, qkv, qkv, kx)


def _merge_kernel(alpha, d_model, oa_ref, za_ref, ob_ref, zb_ref, g_ref, x_ref,
                  wa_ref, wb_ref, wo_ref, lng_ref, lnb_ref, o_ref, ob16_ref):
    up_a = jnp.dot(oa_ref[...] * za_ref[...], wa_ref[...], preferred_element_type=_F32)
    up_b = jnp.dot(ob_ref[...] * zb_ref[...], wb_ref[...], preferred_element_type=_F32)
    mix = (g_ref[:, :d_model].astype(_F32) * up_a + g_ref[:, d_model:].astype(_F32) * up_b)
    y = jnp.dot(mix.astype(_BF16), wo_ref[...], preferred_element_type=_F32)
    r = alpha * x_ref[...] + y
    mu = jnp.mean(r, axis=-1, keepdims=True)
    cen = r - mu
    var = jnp.mean(cen * cen, axis=-1, keepdims=True)
    out = cen * lax.rsqrt(var + LN_EPS) * lng_ref[...] + lnb_ref[...]
    o_ref[...] = out
    ob16_ref[...] = out.astype(_BF16)


def _merge(oa, sza, ob, szb, gates, x, w_up_a, w_up_b, w_out, ln_g, ln_b, alpha):
    m, d_model = x.shape
    w = oa.shape[1]
    tm = _pick(m, (256, 128))
    row = lambda n: pl.BlockSpec((tm, n), lambda i: (i, 0))
    const = lambda a, c: pl.BlockSpec((a, c), lambda i: (0, 0), pipeline_mode=pl.Buffered(1))
    return pl.pallas_call(
        functools.partial(_merge_kernel, alpha, d_model),
        grid=(m // tm,),
        in_specs=[row(w), row(w), row(w), row(w), row(2 * d_model), row(d_model),
                  const(w, d_model), const(w, d_model), const(d_model, d_model),
                  const(1, d_model), const(1, d_model)],
        out_specs=[row(d_model), row(d_model)],
        out_shape=[jax.ShapeDtypeStruct((m, d_model), _F32),
                   jax.ShapeDtypeStruct((m, d_model), _BF16)],
        compiler_params=_params(48, 1),
        name="merge_deepnorm",
    )(oa, sza, ob, szb, gates, x, w_up_a, w_up_b, w_out, ln_g, ln_b)


def _rope_tables(seq):
    half = HEAD_DIM // 2
    inv_freq = ROPE_THETA ** (-jnp.arange(half, dtype=_F32) / half)
    ang = jnp.arange(seq, dtype=_F32)[:, None] * inv_freq[None, :]
    cos, sin = jnp.cos(ang), jnp.sin(ang)
    return jnp.concatenate([cos, cos], axis=1), jnp.concatenate([-sin, sin], axis=1)


def _layer(layer, x, xb, tables, w_in, b_forget, b_gate, w_up_a, w_up_b, w_out, ln_g, ln_b, dims):
    batch, seq, d_model, width_a, width_b, alpha = dims
    heads_a, heads_b = width_a // HEAD_DIM, width_b // HEAD_DIM
    m = batch * seq
    q_scale = LOG2E * HEAD_DIM ** -0.5
    o_za, o_qb = 3 * width_a, 4 * width_a
    o_zb, o_f = o_qb + 3 * width_b, o_qb + 4 * width_b
    o_g = o_f + heads_b
    ones = lambda n: jnp.ones((1, n), _F32)
    project = functools.partial(_project, xb=xb, w=w_in, layer=layer, seq=seq)

    scale_a = jnp.concatenate([jnp.full((1, width_a), q_scale, _F32), ones(2 * width_a)], axis=1)
    qkv_a = project("rope", col0=0, n=3 * width_a, extras=(*tables, scale_a), out_dtype=_F32,
                    rope_cols=2 * width_a)
    sz_a = project("silu", col0=o_za, n=width_a, extras=(), out_dtype=_BF16)
    out_a = _dilated_attention(qkv_a, heads_a)

    scale_b = jnp.concatenate([jnp.full((1, width_b), q_scale, _F32), ones(2 * width_b)], axis=1)
    qkv_b = project("linear", col0=o_qb, n=3 * width_b, extras=(scale_b,), out_dtype=_BF16)
    sz_b = project("silu", col0=o_zb, n=width_b, extras=(), out_dtype=_BF16)
    w_f = jnp.pad(w_in[layer, :, o_f:o_g], ((0, 0), (0, LANES - heads_b)))
    b_f = jnp.pad(b_forget, (0, LANES - heads_b))[None, :]
    kx = _forget_bias(xb, w_f, b_f, batch, seq, heads_b)
    out_b = _forgetting_attention(qkv_b, kx.reshape(batch, seq, LANES), heads_b)

    w_g = w_in[layer, :, o_g:o_g + 2 * d_model][None]
    gates = _project("gate", xb, w_g, 0, 0, 2 * d_model, (b_gate[None, :],), _BF16, seq)
    return _merge(out_a.reshape(m, width_a), sz_a, out_b.reshape(m, width_b), sz_b, gates, x,
                  w_up_a.astype(_BF16), w_up_b.astype(_BF16), w_out.astype(_BF16),
                  ln_g[None, :], ln_b[None, :], alpha)


def kernel(x, w_in, b_forget, b_gate, w_up_a, w_up_b, w_out, ln_g, ln_b):
    batch, seq, d_model = x.shape
    depth = w_in.shape[0]
    width_a, width_b = w_up_a.shape[1], w_up_b.shape[1]
    alpha = float((2 * depth) ** 0.25)
    dims = (batch, seq, d_model, width_a, width_b, alpha)
    tables = _rope_tables(seq)
    x2 = x.reshape(batch * seq, d_model)
    xb = x2.astype(_BF16)
    for l in range(depth):
        x2, xb = _layer(l, x2, xb, tables, w_in, b_forget[l], b_gate[l],
                        w_up_a[l], w_up_b[l], w_out[l], ln_g[l], ln_b[l], dims)
    return x2.reshape(batch, seq, d_model)
```

```python
import functools
import math

import jax
import jax.numpy as jnp
from jax import lax
from jax.experimental import pallas as pl
from jax.experimental.pallas import tpu as pltpu

HEAD_DIM = 128
Q_BLOCK = 128
DILATED_PATTERNS = ((128, 1), (512, 4), (2048, 16))
ROPE_THETA = 10000.0
LN_EPS = 1e-5
LANES = 128
NEG_INF = float("-inf")
LOG2E = math.log2(math.e)
BIAS_PIECES = 3

_F32 = jnp.float32
_BF16 = jnp.bfloat16
_NT = (((1,), (1,)), ((), ()))


def _pick(n, candidates):
    for c in candidates:
        if n % c == 0:
            return c
    raise ValueError(f"no tile in {candidates} divides {n}")


def _params(vmem_mib, n_axes):
    return pltpu.CompilerParams(dimension_semantics=("arbitrary",) * n_axes,
                                vmem_limit_bytes=vmem_mib * 1024 * 1024)


def _lane_chunks(a):
    return [a[:, c * LANES:(c + 1) * LANES] for c in range(a.shape[1] // LANES)]


def _split_bf16(a):
    hi = a.astype(_BF16).astype(_F32)
    mid = (a - hi).astype(_BF16).astype(_F32)
    return hi, mid, a - hi - mid


PROJ_SUB_ROWS = 256


def _proj_kernel(kind, rope_tiles, x_ref, w_ref, *rest):
    o_ref = rest[-1]
    w = w_ref[0].astype(_BF16)
    if kind == "rope":
        cos_ref, sin_ref, scale_ref = rest[:3]
        on = jnp.where(pl.program_id(1) < rope_tiles, 1.0, 0.0)
    def put_heads(rows, c, val):
        o_ref[c, rows, :] = val.astype(o_ref.dtype)

    for rb in range(x_ref.shape[0] // PROJ_SUB_ROWS):
        rows = slice(rb * PROJ_SUB_ROWS, (rb + 1) * PROJ_SUB_ROWS)
        acc = jnp.dot(x_ref[rows, :], w, preferred_element_type=_F32)
        if kind == "rope":
            cos = 1.0 + on * (cos_ref[rows, :] - 1.0)
            sin = on * sin_ref[rows, :]
            for c, t in enumerate(_lane_chunks(acc)):
                r = t * cos + pltpu.roll(t, HEAD_DIM // 2, axis=1) * sin
                put_heads(rows, c, r * scale_ref[:, c * LANES:(c + 1) * LANES])
        elif kind == "linear":
            for c, t in enumerate(_lane_chunks(acc * rest[0][...])):
                put_heads(rows, c, t)
        elif kind == "silu":
            o_ref[rows, :] = (acc * jax.nn.sigmoid(acc)).astype(o_ref.dtype)
        elif kind == "gate":
            o_ref[rows, :] = jax.nn.sigmoid(acc + rest[0][...]).astype(o_ref.dtype)
        else:
            raise ValueError(kind)


def _project(kind, xb, w, layer, col0, n, extras, out_dtype, seq, rope_cols=0):
    m, k = xb.shape
    tm = _pick(seq, (2048, 1024, 512, 256))
    tn = _pick(math.gcd(math.gcd(n, rope_cols), col0), (1024, 768, 512, 256, 128))
    j0 = col0 // tn
    nrow = seq // tm
    if kind in ("rope", "linear"):
        out_spec = pl.BlockSpec((None, tn // LANES, tm, LANES), lambda i, j: (i // nrow, j, i % nrow, 0))
        out_shape = jax.ShapeDtypeStruct((m // seq, n // LANES, seq, LANES), out_dtype)
    else:
        out_spec = pl.BlockSpec((tm, tn), lambda i, j: (i, j))
        out_shape = jax.ShapeDtypeStruct((m, n), out_dtype)
    in_specs = [pl.BlockSpec((tm, k), lambda i, j: (i, 0)),
                pl.BlockSpec((1, k, tn), lambda i, j: (layer, 0, j0 + j))]
    if kind == "rope":
        table = pl.BlockSpec((tm, HEAD_DIM), lambda i, j: (i % nrow, 0))
        in_specs += [table, table, pl.BlockSpec((1, tn), lambda i, j: (0, j))]
    else:
        in_specs += [pl.BlockSpec((1, tn), lambda i, j: (0, j)) for _ in extras]
    return pl.pallas_call(
        functools.partial(_proj_kernel, kind, rope_cols // tn),
        grid=(m // tm, n // tn),
        in_specs=in_specs,
        out_specs=out_spec,
        out_shape=out_shape,
        compiler_params=_params(56, 2),
        name=f"proj_{kind}",
    )(xb, w, *extras)


def _forget_kernel(tiles_per_seq, n_heads, x_ref, w_ref, b_ref, tri_ref, o_ref, carry_ref):
    @pl.when(pl.program_id(0) % tiles_per_seq == 0)
    def _():
        carry_ref[...] = jnp.zeros_like(carry_ref)

    z = jnp.dot(x_ref[...], w_ref[...].astype(_BF16), preferred_element_type=_F32) + b_ref[...]
    log_f = jnp.minimum(z, 0.0) - jnp.log1p(jnp.exp(-jnp.abs(z)))
    tri = tri_ref[...]
    cum = carry_ref[...]
    for piece in _split_bf16(log_f):
        cum = cum + jnp.dot(tri, piece.astype(_BF16), preferred_element_type=_F32)
    carry_ref[...] = cum[-1:, :]
    head = lax.broadcasted_iota(jnp.int32, (LANES, LANES), 0)
    lane = lax.broadcasted_iota(jnp.int32, (LANES, LANES), 1)
    out = jnp.zeros(cum.shape, _F32)
    for k, piece in enumerate(_split_bf16(cum * (-LOG2E))):
        place = ((lane == BIAS_PIECES * head + k) & (head < n_heads)).astype(_BF16)
        out = out + jnp.dot(piece.astype(_BF16), place, preferred_element_type=_F32)
    o_ref[...] = out.astype(_BF16)


def _forget_bias(xb, w_f, b_f, batch, seq, n_heads):
    assert BIAS_PIECES * n_heads <= LANES
    m, k = xb.shape
    tm = _pick(seq, (512, 256, 128))
    tiles = seq // tm
    tri = (lax.broadcasted_iota(jnp.int32, (tm, tm), 0)
           >= lax.broadcasted_iota(jnp.int32, (tm, tm), 1)).astype(_BF16)
    return pl.pallas_call(
        functools.partial(_forget_kernel, tiles, n_heads),
        grid=(m // tm,),
        in_specs=[pl.BlockSpec((tm, k), lambda i: (i, 0)),
                  pl.BlockSpec((k, LANES), lambda i: (0, 0)),
                  pl.BlockSpec((1, LANES), lambda i: (0, 0)),
                  pl.BlockSpec((tm, tm), lambda i: (0, 0))],
        out_specs=pl.BlockSpec((tm, LANES), lambda i: (i, 0)),
        out_shape=jax.ShapeDtypeStruct((m, LANES), _BF16),
        scratch_shapes=[pltpu.VMEM((1, LANES), _F32)],
        compiler_params=_params(32, 1),
        name="forget_bias",
    )(xb, w_f, b_f, tri)


def _dilated_kernel(tile, q_ref, k_ref, v_ref, o_ref, qs_ref, ks_ref, vs_ref, out_ref, lse_ref, mid_ref,
                    kh_ref, vh_ref):
    t = pl.program_id(2)
    row = lax.broadcasted_iota(jnp.int32, (Q_BLOCK, LANES), 0)
    lane = lax.broadcasted_iota(jnp.int32, (Q_BLOCK, LANES), 1)
    in_prev = lane >= row
    in_cur = lane <= row
    ones = jnp.ones((2 * Q_BLOCK, LANES), _BF16)

    @pl.when(t == 0)
    def _():
        kh_ref[...] = jnp.zeros(kh_ref.shape, _BF16)
        vh_ref[...] = jnp.zeros(vh_ref.shape, _BF16)

    @pl.when(t > 0)
    def _():
        for g, (_, d) in enumerate(DILATED_PATTERNS):
            per = tile // d
            for r in range(d):
                last = pl.ds((r + 1) * per - Q_BLOCK, Q_BLOCK)
                kh_ref[g, pl.ds(r * Q_BLOCK, Q_BLOCK), :] = ks_ref[g, last, :]
                vh_ref[g, pl.ds(r * Q_BLOCK, Q_BLOCK), :] = vs_ref[g, last, :]

    (_, d1), (_, d2), (_, d3) = DILATED_PATTERNS
    assert d1 == 1 and d3 % d2 == 0
    per2, per3, step3 = tile // d2, tile // d3, d3 // d2
    for src_ref, dst_ref in ((q_ref, qs_ref), (k_ref, ks_ref), (v_ref, vs_ref)):
        dst_ref[0] = src_ref[0].astype(_BF16)
        for r2 in range(d2):
            cls = src_ref[0, pl.ds(r2, per2, stride=d2), :]
            mid_ref[pl.ds(r2 * per2, per2), :] = cls
            dst_ref[1, pl.ds(r2 * per2, per2), :] = cls.astype(_BF16)
        for r3 in range(d3):
            cls = mid_ref[pl.ds((r3 % d2) * per2 + r3 // d2, per3, stride=step3), :]
            dst_ref[2, pl.ds(r3 * per3, per3), :] = cls.astype(_BF16)

    for g, (_, d) in enumerate(DILATED_PATTERNS):
        per = tile // d
        for r in range(d):
            for n in range(per // Q_BLOCK):
                base = r * per + n * Q_BLOCK
                q = qs_ref[g, pl.ds(base, Q_BLOCK), :]
                if n > 0:
                    kk = ks_ref[g, pl.ds(base - Q_BLOCK, 2 * Q_BLOCK), :]
                    vv = vs_ref[g, pl.ds(base - Q_BLOCK, 2 * Q_BLOCK), :]
                    prev_ok = in_prev
                else:
                    kept = pl.ds(r * Q_BLOCK, Q_BLOCK)
                    kk = jnp.concatenate([kh_ref[g, kept, :], ks_ref[g, pl.ds(base, Q_BLOCK), :]], axis=0)
                    vv = jnp.concatenate([vh_ref[g, kept, :], vs_ref[g, pl.ds(base, Q_BLOCK), :]], axis=0)
                    prev_ok = in_prev & (t > 0)
                s_prev, s_cur = _lane_chunks(lax.dot_general(q, kk, _NT, preferred_element_type=_F32))
                s_prev = jnp.where(prev_ok, s_prev, NEG_INF)
                s_cur = jnp.where(in_cur, s_cur, NEG_INF)
                m = jnp.broadcast_to(jnp.max(jnp.maximum(s_prev, s_cur), axis=1, keepdims=True), s_cur.shape)
                p = jnp.concatenate([jnp.exp2(s_prev - m), jnp.exp2(s_cur - m)], axis=1).astype(_BF16)
                acc, l = _lane_chunks(jnp.dot(p, jnp.concatenate([vv, ones], axis=1), preferred_element_type=_F32))
                nat = pl.ds(n * Q_BLOCK * d + r, Q_BLOCK) if d == 1 else pl.ds(n * Q_BLOCK * d + r, Q_BLOCK, stride=d)
                out_ref[g, nat, :] = acc / l
                lse_ref[g, nat, :] = m + jnp.log2(l)

    lse_all = jnp.maximum(jnp.maximum(lse_ref[0], lse_ref[1]), lse_ref[2])
    num = jnp.zeros((tile, HEAD_DIM), _F32)
    den = jnp.zeros((tile, LANES), _F32)
    for g in range(len(DILATED_PATTERNS)):
        e = jnp.exp2(lse_ref[g] - lse_all)
        num = num + e * out_ref[g]
        den = den + e
    o_ref[0] = (num / den).astype(o_ref.dtype)


def _dilated_attention(qkv, n_heads):
    b, _, s, _ = qkv.shape
    tile = Q_BLOCK * DILATED_PATTERNS[-1][1]
    assert s % tile == 0
    n_pat = len(DILATED_PATTERNS)
    tiled = lambda part: pl.BlockSpec((None, 1, tile, HEAD_DIM), lambda bi, h, t: (bi, part * n_heads + h, t, 0))
    stats = pltpu.VMEM((n_pat, tile, LANES), _F32)
    staged = pltpu.VMEM((n_pat, tile, HEAD_DIM), _BF16)
    kept = pltpu.VMEM((n_pat, Q_BLOCK * DILATED_PATTERNS[-1][1], HEAD_DIM), _BF16)
    return pl.pallas_call(
        functools.partial(_dilated_kernel, tile),
        grid=(b, n_heads, s // tile),
        in_specs=[tiled(0), tiled(1), tiled(2)],
        out_specs=pl.BlockSpec((1, tile, HEAD_DIM), lambda bi, h, t: (bi, t, h)),
        out_shape=jax.ShapeDtypeStruct((b, s, n_heads * HEAD_DIM), _BF16),
        scratch_shapes=[staged, staged, staged, stats, stats, pltpu.VMEM((tile, HEAD_DIM), _F32), kept, kept],
        compiler_params=_params(40, 3),
        name="dilated_attention",
    )(qkv, qkv, qkv)


ONES_ROWS = 16
FOX_KEY_BLOCK = 512


def _fox_kernel(tq, q_ref, k_ref, v_ref, kx_ref, o_ref, vt_ref, m_ref, acc_ref, sa_ref, sb_ref):
    qi = pl.program_id(2)
    seq = v_ref.shape[1]
    tk = FOX_KEY_BLOCK
    n_sub = tq // tk

    @pl.when(qi == 0)
    def _():
        for c in range(seq // tk):
            blk = v_ref[0, c * tk:(c + 1) * tk, :].astype(_F32)
            vt_ref[:HEAD_DIM, c * tk:(c + 1) * tk] = blk.T.astype(_BF16)
        vt_ref[HEAD_DIM:, :] = jnp.ones((ONES_ROWS, seq), _BF16)

    depth = lax.broadcasted_iota(jnp.int32, (LANES, tk), 0)
    first = BIAS_PIECES * pl.program_id(1)
    ones = ((depth >= first) & (depth < first + BIAS_PIECES)).astype(_BF16)
    q_chunks = {}

    def q_chunk(i):
        if i not in q_chunks:
            q_t = q_ref[0, i * tk:(i + 1) * tk, :].astype(_F32).T.astype(_BF16)
            q_chunks[i] = jnp.concatenate([q_t, ones], axis=0)
        return q_chunks[i]
    m_ref[...] = jnp.full_like(m_ref, NEG_INF)
    acc_ref[...] = jnp.zeros_like(acc_ref)

    def k_block(j):
        k0 = pl.multiple_of(j * tk, tk)
        return jnp.concatenate([k_ref[0, pl.ds(k0, tk), :], kx_ref[0, pl.ds(k0, tk), :]], axis=1)

    def scores_chunk(k_aug, s_ref, q0, c):
        s_ref[:, c * tk:(c + 1) * tk] = jnp.dot(k_aug, q_chunk(q0 // tk + c), preferred_element_type=_F32)

    def update_chunk(vt, s_ref, q0, c, diagonal):
        s = s_ref[:, c * tk:(c + 1) * tk]
        if diagonal and c == 0:
            key = lax.broadcasted_iota(jnp.int32, s.shape, 0)
            qry = lax.broadcasted_iota(jnp.int32, s.shape, 1)
            s = jnp.where(key <= qry, s, NEG_INF)
        cols = slice(q0 + c * tk, q0 + (c + 1) * tk)
        m_prev = m_ref[:, cols]
        m_new = jnp.maximum(m_prev, jnp.max(s, axis=0, keepdims=True))
        a = jnp.exp2(m_prev - m_new)
        p = jnp.exp2(s - m_new).astype(_BF16)
        acc_ref[:, cols] = a * acc_ref[:, cols] + jnp.dot(vt, p, preferred_element_type=_F32)
        m_ref[:, cols] = m_new

    def step(j, s_cur, s_next, q0=0, diagonal=False, q0_next=None):
        vt = vt_ref[:, pl.ds(pl.multiple_of(j * tk, tk), tk)]
        n_cur = (tq - q0) // tk
        n_next = 0 if q0_next is None else (tq - q0_next) // tk
        k_next = k_block(j + 1) if n_next else None
        for c in range(max(n_cur, n_next)):
            if c < n_next:
                scores_chunk(k_next, s_next, q0_next, c)
            if c < n_cur:
                update_chunk(vt, s_cur, q0, c, diagonal)

    def buffers(u):
        return (sa_ref, sb_ref) if u % 2 == 0 else (sb_ref, sa_ref)

    def below_diagonal(jj, carry):
        for u in range(n_sub):
            step(n_sub * jj + u, *buffers(u), q0_next=0)
        return carry

    first_keys = k_block(0)
    for c in range(n_sub):
        scores_chunk(first_keys, sa_ref, 0, c)
    lax.fori_loop(0, qi, below_diagonal, 0)
    for d in range(n_sub):
        step(n_sub * qi + d, *buffers(d), q0=d * tk, diagonal=True,
             q0_next=(d + 1) * tk if d + 1 < n_sub else None)
        cols = slice(d * tk, (d + 1) * tk)
        out_t = acc_ref[:HEAD_DIM, cols] / acc_ref[HEAD_DIM:HEAD_DIM + 1, cols]
        o_ref[0, cols, :] = out_t.T.astype(o_ref.dtype)


def _forgetting_attention(qkv, kx, n_heads):
    b, _, s, _ = qkv.shape
    tq = _pick(s, (4 * FOX_KEY_BLOCK, 2 * FOX_KEY_BLOCK))
    whole = lambda part: pl.BlockSpec((None, 1, s, HEAD_DIM), lambda bi, h, t: (bi, part * n_heads + h, 0, 0))
    return pl.pallas_call(
        functools.partial(_fox_kernel, tq),
        grid=(b, n_heads, s // tq),
        in_specs=[pl.BlockSpec((None, 1, tq, HEAD_DIM), lambda bi, h, t: (bi, h, t, 0)), whole(1), whole(2),
                  pl.BlockSpec((1, s, LANES), lambda bi, h, t: (bi, 0, 0))],
        out_specs=pl.BlockSpec((1, tq, HEAD_DIM), lambda bi, h, t: (bi, t, h)),
        out_shape=jax.ShapeDtypeStruct((b, s, n_heads * HEAD_DIM), _BF16),
        scratch_shapes=[pltpu.VMEM((HEAD_DIM + ONES_ROWS, s), _BF16),
                        pltpu.VMEM((1, tq), _F32),
                        pltpu.VMEM((HEAD_DIM + ONES_ROWS, tq), _F32),
                        pltpu.VMEM((FOX_KEY_BLOCK, tq), _F32),
                        pltpu.VMEM((FOX_KEY_BLOCK, tq), _F32)],
        compiler_params=_params(48, 3),
        name="forgetting_attention",
    )(qkv, qkv, qkv, kx)


def _merge_kernel(alpha, d_model, oa_ref, za_ref, ob_ref, zb_ref, g_ref, x_ref,
                  wa_ref, wb_ref, wo_ref, lng_ref, lnb_ref, o_ref, ob16_ref):
    up_a = jnp.dot(oa_ref[...] * za_ref[...], wa_ref[...], preferred_element_type=_F32)
    up_b = jnp.dot(ob_ref[...] * zb_ref[...], wb_ref[...], preferred_element_type=_F32)
    mix = (g_ref[:, :d_model].astype(_F32) * up_a + g_ref[:, d_model:].astype(_F32) * up_b)
    y = jnp.dot(mix.astype(_BF16), wo_ref[...], preferred_element_type=_F32)
    r = alpha * x_ref[...] + y
    mu = jnp.mean(r, axis=-1, keepdims=True)
    cen = r - mu
    var = jnp.mean(cen * cen, axis=-1, keepdims=True)
    out = cen * lax.rsqrt(var + LN_EPS) * lng_ref[...] + lnb_ref[...]
    o_ref[...] = out
    ob16_ref[...] = out.astype(_BF16)


def _merge(oa, sza, ob, szb, gates, x, w_up_a, w_up_b, w_out, ln_g, ln_b, alpha):
    m, d_model = x.shape
    w = oa.shape[1]
    tm = _pick(m, (256, 128))
    row = lambda n: pl.BlockSpec((tm, n), lambda i: (i, 0))
    const = lambda a, c: pl.BlockSpec((a, c), lambda i: (0, 0), pipeline_mode=pl.Buffered(1))
    return pl.pallas_call(
        functools.partial(_merge_kernel, alpha, d_model),
        grid=(m // tm,),
        in_specs=[row(w), row(w), row(w), row(w), row(2 * d_model), row(d_model),
                  const(w, d_model), const(w, d_model), const(d_model, d_model),
                  const(1, d_model), const(1, d_model)],
        out_specs=[row(d_model), row(d_model)],
        out_shape=[jax.ShapeDtypeStruct((m, d_model), _F32),
                   jax.ShapeDtypeStruct((m, d_model), _BF16)],
        compiler_params=_params(48, 1),
        name="merge_deepnorm",
    )(oa, sza, ob, szb, gates, x, w_up_a, w_up_b, w_out, ln_g, ln_b)


def _rope_tables(seq):
    half = HEAD_DIM // 2
    inv_freq = ROPE_THETA ** (-jnp.arange(half, dtype=_F32) / half)
    ang = jnp.arange(seq, dtype=_F32)[:, None] * inv_freq[None, :]
    cos, sin = jnp.cos(ang), jnp.sin(ang)
    return jnp.concatenate([cos, cos], axis=1), jnp.concatenate([-sin, sin], axis=1)


def _layer(layer, x, xb, tables, w_in, b_forget, b_gate, w_up_a, w_up_b, w_out, ln_g, ln_b, dims):
    batch, seq, d_model, width_a, width_b, alpha = dims
    heads_a, heads_b = width_a // HEAD_DIM, width_b // HEAD_DIM
    m = batch * seq
    q_scale = LOG2E * HEAD_DIM ** -0.5
    o_za, o_qb = 3 * width_a, 4 * width_a
    o_zb, o_f = o_qb + 3 * width_b, o_qb + 4 * width_b
    o_g = o_f + heads_b
    ones = lambda n: jnp.ones((1, n), _F32)
    project = functools.partial(_project, xb=xb, w=w_in, layer=layer, seq=seq)

    scale_a = jnp.concatenate([jnp.full((1, width_a), q_scale, _F32), ones(2 * width_a)], axis=1)
    qkv_a = project("rope", col0=0, n=3 * width_a, extras=(*tables, scale_a), out_dtype=_F32,
                    rope_cols=2 * width_a)
    sz_a = project("silu", col0=o_za, n=width_a, extras=(), out_dtype=_BF16)
    out_a = _dilated_attention(qkv_a, heads_a)

    scale_b = jnp.concatenate([jnp.full((1, width_b), q_scale, _F32), ones(2 * width_b)], axis=1)
    qkv_b = project("linear", col0=o_qb, n=3 * width_b, extras=(scale_b,), out_dtype=_BF16)
    sz_b = project("silu", col0=o_zb, n=width_b, extras=(), out_dtype=_BF16)
    w_f = jnp.pad(w_in[layer, :, o_f:o_g], ((0, 0), (0, LANES - heads_b)))
    b_f = jnp.pad(b_forget, (0, LANES - heads_b))[None, :]
    kx = _forget_bias(xb, w_f, b_f, batch, seq, heads_b)
    out_b = _forgetting_attention(qkv_b, kx.reshape(batch, seq, LANES), heads_b)

    w_g = w_in[layer, :, o_g:o_g + 2 * d_model][None]
    gates = _project("gate", xb, w_g, 0, 0, 2 * d_model, (b_gate[None, :],), _BF16, seq)
    return _merge(out_a.reshape(m, width_a), sz_a, out_b.reshape(m, width_b), sz_b, gates, x,
                  w_up_a.astype(_BF16), w_up_b.astype(_BF16), w_out.astype(_BF16),
                  ln_g[None, :], ln_b[None, :], alpha)


def kernel(x, w_in, b_forget, b_gate, w_up_a, w_up_b, w_out, ln_g, ln_b):
    batch, seq, d_model = x.shape
    depth = w_in.shape[0]
    width_a, width_b = w_up_a.shape[1], w_up_b.shape[1]
    alpha = float((2 * depth) ** 0.25)
    dims = (batch, seq, d_model, width_a, width_b, alpha)
    tables = _rope_tables(seq)
    x2 = x.reshape(batch * seq, d_model)
    xb = x2.astype(_BF16)
    for l in range(depth):
        x2, xb = _layer(l, x2, xb, tables, w_in, b_forget[l], b_gate[l],
                        w_up_a[l], w_up_b[l], w_out[l], ln_g[l], ln_b[l], dims)
    return x2.reshape(batch, seq, d_model)
```
